```python
import jax, jax.numpy as jnp
from jax import lax
import numpy as np

D_MODEL = 1024
BATCH = 16
SEQ = 4096
DEPTH = 2

EXPAND = 2
D_INNER = EXPAND * D_MODEL
EPS = 1e-6

CONV_WIDTH = D_INNER // 2
ATTN_WIDTH = D_INNER - CONV_WIDTH
SB_HEAD_DIM = 128
SB_HEADS = ATTN_WIDTH // SB_HEAD_DIM
CONF_KERNEL = 31
Q_BLOCK = 128
EVEN_SPLITS = [CONV_WIDTH, 2 * CONV_WIDTH, 3 * CONV_WIDTH,
               3 * CONV_WIDTH + ATTN_WIDTH, 3 * CONV_WIDTH + 2 * ATTN_WIDTH,
               3 * CONV_WIDTH + 3 * ATTN_WIDTH]
IN_EVEN = 3 * CONV_WIDTH + 4 * ATTN_WIDTH

SSM_HEAD_DIM = 64
SSM_HEADS = D_INNER // SSM_HEAD_DIM
SSM_GROUPS = 4
SSM_STATE = 128
SSM_CONV = 4
SSM_CHUNK = 128
XBC_WIDTH = D_INNER + 2 * SSM_GROUPS * SSM_STATE
IN_ODD = D_INNER + XBC_WIDTH + SSM_HEADS
DT_MIN = 0.001
DT_MAX = 0.1

N_EVEN = (DEPTH + 1) // 2
N_ODD = DEPTH // 2

kernel_name = "hybrid_conformer_stickbreak_ssd"


def rmsnorm(x, w):
    xf = x.astype(jnp.float32)
    y = xf * lax.rsqrt(jnp.mean(xf * xf, axis=-1, keepdims=True) + EPS)
    return (y * w.astype(jnp.float32)).astype(x.dtype)


def layernorm(x, w, b):
    xf = x.astype(jnp.float32)
    mu = jnp.mean(xf, axis=-1, keepdims=True)
    xc = xf - mu
    var = jnp.mean(xc * xc, axis=-1, keepdims=True)
    return xc * lax.rsqrt(var + EPS) * w.astype(jnp.float32) + b.astype(jnp.float32)


def causal_dwconv(x, w, b):
    k_width = w.shape[0]
    out = lax.conv_general_dilated(
        x, w[:, None, :].astype(x.dtype), window_strides=(1,),
        padding=((k_width - 1, 0),), dimension_numbers=('NWC', 'WIO', 'NWC'),
        feature_group_count=x.shape[-1])
    return out + b.astype(x.dtype)


def stick_breaking_attention(q, k, v):
    S = q.shape[1]
    dh = q.shape[-1]
    qf = q.astype(jnp.float32) * (dh ** -0.5)
    kf = k.astype(jnp.float32)
    vf = v.astype(jnp.float32)
    outs = []
    for i in range(S // Q_BLOCK):
        t0 = i * Q_BLOCK
        kl = t0 + Q_BLOCK
        z = jnp.einsum('bthd,bshd->bhts', qf[:, t0:kl], kf[:, :kl])
        t_idx = t0 + jnp.arange(Q_BLOCK)
        s_idx = jnp.arange(kl)
        mask = s_idx[None, :] < t_idx[:, None]
        log_keep = jnp.where(mask, jax.nn.log_sigmoid(-z), 0.0)
        later = lax.cumsum(log_keep, axis=3, reverse=True) - log_keep
        wts = jnp.where(mask, jnp.exp(jax.nn.log_sigmoid(z) + later), 0.0)
        outs.append(jnp.einsum('bhts,bshd->bthd', wts, vf[:, :kl]))
    return jnp.concatenate(outs, axis=1).astype(q.dtype)


def ssd_scan(x, dt, a, bm, cm):
    bsz, S, H, P = x.shape
    G, N = bm.shape[2], bm.shape[3]
    R = H // G
    L = SSM_CHUNK
    nc = S // L
    xs = (x * dt[..., None]).reshape(bsz, nc, L, G, R, P).transpose(1, 0, 2, 3, 4, 5)
    la = (dt * a).reshape(bsz, nc, L, G, R).transpose(1, 0, 3, 4, 2)
    bc = bm.reshape(bsz, nc, L, G, N).transpose(1, 0, 2, 3, 4)
    cc = cm.reshape(bsz, nc, L, G, N).transpose(1, 0, 2, 3, 4)
    causal = jnp.tril(jnp.ones((L, L), dtype=bool))

    def step(state, inp):
        xck, ack, bck, cck = inp
        cs = jnp.cumsum(ack, axis=-1)
        seg = cs[..., :, None] - cs[..., None, :]
        decay = jnp.exp(jnp.where(causal, seg, -jnp.inf))
        cb = jnp.einsum('blgn,bsgn->bgls', cck, bck)
        y = jnp.einsum('bgls,bgrls,bsgrp->blgrp', cb, decay, xck)
        y = y + jnp.einsum('blgn,bgrpn,bgrl->blgrp', cck, state, jnp.exp(cs))
        tail = jnp.exp(cs[..., -1:] - cs)
        new_state = (state * jnp.exp(cs[..., -1])[..., None, None]
                     + jnp.einsum('bsgn,bgrs,bsgrp->bgrpn', bck, tail, xck))
        return new_state, y

    init = jnp.zeros((bsz, G, R, P, N), jnp.float32)
    _, ys = lax.scan(step, init, (xs, la, bc, cc))
    return ys.transpose(1, 0, 2, 3, 4, 5).reshape(bsz, S, H, P)


def conv_attn_mixer(h, w_in, dw_w, dw_b, ln_w, ln_b, w_out):
    bsz, S, _ = h.shape
    proj = h @ w_in
    glu_a, glu_b, gate_c, q, k, v, gate_a = jnp.split(proj, EVEN_SPLITS, axis=-1)
    u = glu_a * jax.nn.sigmoid(glu_b)
    u = causal_dwconv(u, dw_w, dw_b)
    u = jax.nn.silu(layernorm(u, ln_w, ln_b))
    y_conv = (u * jax.nn.silu(gate_c.astype(jnp.float32))).astype(h.dtype)
    shp = (bsz, S, SB_HEADS, SB_HEAD_DIM)
    o = stick_breaking_attention(q.reshape(shp), k.reshape(shp), v.reshape(shp))
    y_attn = (o.reshape(bsz, S, ATTN_WIDTH).astype(jnp.float32)
              * jax.nn.silu(gate_a.astype(jnp.float32))).astype(h.dtype)
    return jnp.concatenate([y_conv, y_attn], axis=-1) @ w_out


def mamba2_mixer(h, w_in, conv_w, conv_b, dt_bias, a_log, d_skip, norm_w, w_out):
    bsz, S, _ = h.shape
    gn = SSM_GROUPS * SSM_STATE
    proj = h @ w_in
    z, xbc, dt = jnp.split(proj, [D_INNER, D_INNER + XBC_WIDTH], axis=-1)
    xbc = jax.nn.silu(causal_dwconv(xbc, conv_w, conv_b))
    xs, bm, cm = jnp.split(xbc, [D_INNER, D_INNER + gn], axis=-1)
    xs = xs.astype(jnp.float32).reshape(bsz, S, SSM_HEADS, SSM_HEAD_DIM)
    bm = bm.astype(jnp.float32).reshape(bsz, S, SSM_GROUPS, SSM_STATE)
    cm = cm.astype(jnp.float32).reshape(bsz, S, SSM_GROUPS, SSM_STATE)
    dt = jax.nn.softplus(dt.astype(jnp.float32) + dt_bias.astype(jnp.float32))
    a = -jnp.exp(a_log.astype(jnp.float32))
    y = ssd_scan(xs, dt, a, bm, cm)
    y = y + d_skip.astype(jnp.float32)[:, None] * xs
    y = y.reshape(bsz, S, D_INNER) * jax.nn.silu(z.astype(jnp.float32))
    yg = y.reshape(bsz, S, SSM_GROUPS, D_INNER // SSM_GROUPS)
    yg = yg * lax.rsqrt(jnp.mean(yg * yg, axis=-1, keepdims=True) + EPS)
    y = yg.reshape(bsz, S, D_INNER) * norm_w.astype(jnp.float32)
    return y.astype(h.dtype) @ w_out


def setup_inputs(seed: int = 0) -> dict:
    key = jax.random.key(seed)
    ks = jax.random.split(key, 20)
    f32 = jnp.float32
    nrm = lambda k, shp, s: jax.random.normal(k, shp, f32) * s
    x = jax.random.normal(ks[0], (BATCH, SEQ, D_MODEL), f32)
    ev_norm_w = 1.0 + nrm(ks[1], (N_EVEN, D_MODEL), 0.02)
    ev_w_in = nrm(ks[2], (N_EVEN, D_MODEL, IN_EVEN), D_MODEL ** -0.5)
    ev_dw_w = nrm(ks[3], (N_EVEN, CONF_KERNEL, CONV_WIDTH), CONF_KERNEL ** -0.5)
    ev_dw_b = nrm(ks[4], (N_EVEN, CONV_WIDTH), 0.02)
    ev_ln_w = 1.0 + nrm(ks[5], (N_EVEN, CONV_WIDTH), 0.02)
    ev_ln_b = nrm(ks[6], (N_EVEN, CONV_WIDTH), 0.02)
    ev_w_out = nrm(ks[7], (N_EVEN, D_INNER, D_MODEL), D_INNER ** -0.5)
    od_norm_w = 1.0 + nrm(ks[8], (N_ODD, D_MODEL), 0.02)
    od_w_in = nrm(ks[9], (N_ODD, D_MODEL, IN_ODD), D_MODEL ** -0.5)
    od_conv_w = nrm(ks[10], (N_ODD, SSM_CONV, XBC_WIDTH), SSM_CONV ** -0.5)
    od_conv_b = nrm(ks[11], (N_ODD, XBC_WIDTH), 0.02)
    u = jax.random.uniform(ks[12], (N_ODD, SSM_HEADS), f32)
    dt0 = jnp.exp(u * (np.log(DT_MAX) - np.log(DT_MIN)) + np.log(DT_MIN))
    od_dt_bias = dt0 + jnp.log(-jnp.expm1(-dt0))
    od_a_log = jnp.log(jax.random.uniform(ks[13], (N_ODD, SSM_HEADS), f32, 1.0, 16.0))
    od_d = 1.0 + nrm(ks[14], (N_ODD, SSM_HEADS), 0.1)
    od_gnorm_w = 1.0 + nrm(ks[15], (N_ODD, D_INNER), 0.02)
    od_w_out = nrm(ks[16], (N_ODD, D_INNER, D_MODEL), D_INNER ** -0.5)
    final_norm_w = 1.0 + nrm(ks[17], (D_MODEL,), 0.02)
    return {"x": x, "ev_norm_w": ev_norm_w, "ev_w_in": ev_w_in, "ev_dw_w": ev_dw_w,
            "ev_dw_b": ev_dw_b, "ev_ln_w": ev_ln_w, "ev_ln_b": ev_ln_b, "ev_w_out": ev_w_out,
            "od_norm_w": od_norm_w, "od_w_in": od_w_in, "od_conv_w": od_conv_w,
            "od_conv_b": od_conv_b, "od_dt_bias": od_dt_bias, "od_a_log": od_a_log,
            "od_d": od_d, "od_gnorm_w": od_gnorm_w, "od_w_out": od_w_out,
            "final_norm_w": final_norm_w}


def reference(x, ev_norm_w, ev_w_in, ev_dw_w, ev_dw_b, ev_ln_w, ev_ln_b, ev_w_out,
              od_norm_w, od_w_in, od_conv_w, od_conv_b, od_dt_bias, od_a_log, od_d,
              od_gnorm_w, od_w_out, final_norm_w):
    h = x
    for layer in range(DEPTH):
        i = layer // 2
        if layer % 2 == 0:
            h = h + conv_attn_mixer(rmsnorm(h, ev_norm_w[i]), ev_w_in[i], ev_dw_w[i],
                                    ev_dw_b[i], ev_ln_w[i], ev_ln_b[i], ev_w_out[i])
        else:
            h = h + mamba2_mixer(rmsnorm(h, od_norm_w[i]), od_w_in[i], od_conv_w[i],
                                 od_conv_b[i], od_dt_bias[i], od_a_log[i], od_d[i],
                                 od_gnorm_w[i], od_w_out[i])
    return rmsnorm(h, final_norm_w)
```

```python
import functools

import jax
import jax.numpy as jnp
from jax import lax
from jax.experimental import pallas as pl
from jax.experimental.pallas import tpu as pltpu

F32 = jnp.float32
BF16 = jnp.bfloat16

EPS = 1e-6
LANES = 128
SB_HEAD_DIM = 128
CONF_KERNEL = 31
SSM_HEAD_DIM = 64
SSM_HEADS = 32
SSM_GROUPS = 4
SSM_STATE = 128
SSM_CONV = 4
SSM_CHUNK = 128
VMEM_LIMIT = 56 * 1024 * 1024


def _cparams(*sem):
    return pltpu.CompilerParams(dimension_semantics=sem, vmem_limit_bytes=VMEM_LIMIT)


def _dot(a, b):
    return jnp.dot(a, b, preferred_element_type=F32)


def _dot_nt(a, b):
    return lax.dot_general(a, b, (((1,), (1,)), ((), ())), preferred_element_type=F32)


def _silu(x):
    return x * jax.nn.sigmoid(x)


def _softplus(x):
    return jnp.maximum(x, 0.0) + jnp.log(1.0 + jnp.exp(-jnp.abs(x)))


def _rmsnorm_rows(x_ref, nw_ref, xn_ref, rows, step=256):
    for r in range(0, rows, step):
        x = x_ref[r:r + step, :]
        ms = jnp.mean(x * x, axis=-1, keepdims=True)
        xn_ref[r:r + step, :] = (x * lax.rsqrt(ms + EPS) * nw_ref[...]).astype(BF16)


def _proj_kernel(x_ref, nw_ref, w_ref, o_ref, xn_ref, *, act):
    @pl.when(pl.program_id(1) == 0)
    def _():
        _rmsnorm_rows(x_ref, nw_ref, xn_ref, x_ref.shape[0])

    acc = _dot(xn_ref[...], w_ref[...])
    if act == "silu":
        acc = _silu(acc)
    o_ref[...] = acc.astype(o_ref.dtype)


def _proj_glu_kernel(x_ref, nw_ref, wa_ref, wb_ref, o_ref, xn_ref):
    @pl.when(pl.program_id(1) == 0)
    def _():
        _rmsnorm_rows(x_ref, nw_ref, xn_ref, x_ref.shape[0])

    xn = xn_ref[...]
    a = _dot(xn, wa_ref[...])
    b = _dot(xn, wb_ref[...])
    o_ref[...] = (a * jax.nn.sigmoid(b)).astype(o_ref.dtype)


def _norm_proj(x2d, norm_w, w, *, act, out_dtype, tm, tn, name):
    t, d = x2d.shape
    n = w.shape[1]
    tn = min(tn, n)
    return pl.pallas_call(
        functools.partial(_proj_kernel, act=act),
        grid=(t // tm, n // tn),
        in_specs=[
            pl.BlockSpec((tm, d), lambda i, j: (i, 0)),
            pl.BlockSpec((1, d), lambda i, j: (0, 0)),
            pl.BlockSpec((d, tn), lambda i, j: (0, j)),
        ],
        out_specs=pl.BlockSpec((tm, tn), lambda i, j: (i, j)),
        out_shape=jax.ShapeDtypeStruct((t, n), out_dtype),
        scratch_shapes=[pltpu.VMEM((tm, d), BF16)],
        compiler_params=_cparams("parallel", "arbitrary"),
        name=name,
    )(x2d, norm_w.reshape(1, d), w)


def _norm_proj_glu(x2d, norm_w, wa, wb, *, tm, tn, name):
    t, d = x2d.shape
    n = wa.shape[1]
    tn = min(tn, n)
    return pl.pallas_call(
        _proj_glu_kernel,
        grid=(t // tm, n // tn),
        in_specs=[
            pl.BlockSpec((tm, d), lambda i, j: (i, 0)),
            pl.BlockSpec((1, d), lambda i, j: (0, 0)),
            pl.BlockSpec((d, tn), lambda i, j: (0, j)),
            pl.BlockSpec((d, tn), lambda i, j: (0, j)),
        ],
        out_specs=pl.BlockSpec((tm, tn), lambda i, j: (i, j)),
        out_shape=jax.ShapeDtypeStruct((t, n), F32),
        scratch_shapes=[pltpu.VMEM((tm, d), BF16)],
        compiler_params=_cparams("parallel", "arbitrary"),
        name=name,
    )(x2d, norm_w.reshape(1, d), wa, wb)


def _outproj2_kernel(ya_ref, yb_ref, wa_ref, wb_ref, res_ref, o_ref):
    acc = _dot(ya_ref[...], wa_ref[...]) + _dot(yb_ref[...], wb_ref[...])
    o_ref[...] = acc + res_ref[...]


def _outproj_norm_kernel(y_ref, w_ref, res_ref, nw_ref, o_ref):
    h = _dot(y_ref[...], w_ref[...]) + res_ref[...]
    ms = jnp.mean(h * h, axis=-1, keepdims=True)
    o_ref[...] = h * lax.rsqrt(ms + EPS) * nw_ref[...]


CONV_HALO = 32
CONV_ROWS = 32


def _conv_branch_kernel(u_ref, halo_ref, g_ref, w_ref, b_ref, lnw_ref, lnb_ref, o_ref, win_ref):
    ts = u_ref.shape[0]
    first = pl.program_id(1) == 0
    win_ref[0:CONV_HALO, :] = jnp.where(first, 0.0, halo_ref[...])
    win_ref[CONV_HALO:, :] = u_ref[...]
    off = CONV_HALO - (CONF_KERNEL - 1)
    for r in range(0, ts, CONV_ROWS):
        acc = jnp.broadcast_to(b_ref[...], (CONV_ROWS, u_ref.shape[1]))
        for k in range(CONF_KERNEL):
            acc = acc + win_ref[r + off + k:r + off + k + CONV_ROWS, :] * w_ref[k:k + 1, :]
        mu = jnp.mean(acc, axis=-1, keepdims=True)
        xc = acc - mu
        var = jnp.mean(xc * xc, axis=-1, keepdims=True)
        y = xc * lax.rsqrt(var + EPS) * lnw_ref[...] + lnb_ref[...]
        o_ref[r:r + CONV_ROWS, :] = (_silu(y) * g_ref[r:r + CONV_ROWS, :]).astype(o_ref.dtype)


def _conv_branch(u, gates, dw_w, dw_b, ln_w, ln_b, *, ts):
    b, s, c = u.shape
    hb = ts // CONV_HALO
    w_pad = jnp.zeros((32, c), F32).at[:CONF_KERNEL].set(dw_w)
    return pl.pallas_call(
        _conv_branch_kernel,
        grid=(b, s // ts),
        in_specs=[
            pl.BlockSpec((None, ts, c), lambda bi, i: (bi, i, 0)),
            pl.BlockSpec((None, CONV_HALO, c), lambda bi, i: (bi, jnp.maximum(i * hb - 1, 0), 0)),
            pl.BlockSpec((None, ts, c), lambda bi, i: (bi, i, 0)),
            pl.BlockSpec((32, c), lambda bi, i: (0, 0)),
            pl.BlockSpec((1, c), lambda bi, i: (0, 0)),
            pl.BlockSpec((1, c), lambda bi, i: (0, 0)),
            pl.BlockSpec((1, c), lambda bi, i: (0, 0)),
        ],
        out_specs=pl.BlockSpec((None, ts, c), lambda bi, i: (bi, i, 0)),
        out_shape=jax.ShapeDtypeStruct((b, s, c), BF16),
        scratch_shapes=[pltpu.VMEM((ts + CONV_HALO, c), F32)],
        compiler_params=_cparams("parallel", "arbitrary"),
        name="conv_branch",
    )(u, u, gates, w_pad, dw_b.reshape(1, c), ln_w.reshape(1, c), ln_b.reshape(1, c))


def _attn_kernel(q_ref, k_ref, v_ref, g_ref, o_ref, acc_ref, carry_ref, *, tq, tk):
    i = pl.program_id(2)
    acc_ref[...] = jnp.zeros_like(acc_ref)
    carry_ref[...] = jnp.zeros_like(carry_ref)
    rr = lax.broadcasted_iota(jnp.int32, (tk, tk), 0)
    cc = lax.broadcasted_iota(jnp.int32, (tk, tk), 1)
    upper = jnp.where(rr > cc, 1.0, 0.0).astype(BF16)

    def tile(kstart, lo, diag):
        m = tq - lo
        q = q_ref[lo:tq, :]
        k = k_ref[pl.ds(kstart, tk), :]
        v = v_ref[pl.ds(kstart, tk), :]
        z = _dot_nt(q, k)
        sp = _softplus(z)
        lsig = z - sp
        if diag:
            mask = (lax.broadcasted_iota(jnp.int32, (m, tk), 1)
                    < lax.broadcasted_iota(jnp.int32, (m, tk), 0))
            spm = jnp.where(mask, sp, 0.0)
        else:
            spm = sp
        carry = carry_ref[lo:tq, :]
        later = jnp.tile(carry, (1, tk // LANES)) - _dot(spm.astype(BF16), upper)
        w = jnp.exp(lsig + later)
        if diag:
            w = jnp.where(mask, w, 0.0)
        acc_ref[lo:tq, :] += _dot(w.astype(BF16), v)
        carry_ref[lo:tq, :] = carry - jnp.sum(spm, axis=1, keepdims=True)

    base = pl.multiple_of(i * tq, tq)
    for lo in range(tq - tk, -1, -tk):
        tile(base + lo, lo, True)

    nfull = i * (tq // tk)

    def body(jj, c):
        j = nfull - 1 - jj
        tile(pl.multiple_of(j * tk, tk), 0, False)
        return c

    lax.fori_loop(0, nfull, body, 0)
    o_ref[...] = (acc_ref[...] * g_ref[...]).astype(o_ref.dtype)


def _attention(qkv, gates, *, heads, tq, tk):
    b, s, _ = qkv.shape
    d = SB_HEAD_DIM
    return pl.pallas_call(
        functools.partial(_attn_kernel, tq=tq, tk=tk),
        grid=(b, heads, s // tq),
        in_specs=[
            pl.BlockSpec((None, tq, d), lambda bi, h, i: (bi, i, h)),
            pl.BlockSpec((None, s, d), lambda bi, h, i: (bi, 0, heads + h)),
            pl.BlockSpec((None, s, d), lambda bi, h, i: (bi, 0, 2 * heads + h)),
            pl.BlockSpec((None, tq, d), lambda bi, h, i: (bi, i, heads + h)),
        ],
        out_specs=pl.BlockSpec((None, tq, d), lambda bi, h, i: (bi, i, h)),
        out_shape=jax.ShapeDtypeStruct((b, s, heads * d), BF16),
        scratch_shapes=[pltpu.VMEM((tq, d), F32), pltpu.VMEM((tq, LANES), F32)],
        compiler_params=_cparams("parallel", "parallel", "arbitrary"),
        name="sb_attention",
    )(qkv, qkv, qkv, gates)


SSD_HALO = 8


def _split3(x):
    hi = x.astype(BF16)
    r = x - hi.astype(F32)
    mid = r.astype(BF16)
    lo = (r - mid.astype(F32)).astype(BF16)
    return hi, mid, lo


def _ssd_kernel(xbc_ref, halo_ref, dt_ref, sz_ref, cw_ref, cb_ref, dtb_ref, alog_ref, dskip_ref,
                nw_ref, o_ref, win_ref, xc_ref, y_ref, st_ref):
    L = SSM_CHUNK
    P = SSM_HEAD_DIM
    N = SSM_STATE
    d_inner = SSM_HEADS * P
    gn = SSM_GROUPS * N
    hpg = SSM_HEADS // SSM_GROUPS
    c = pl.program_id(1)

    @pl.when(c == 0)
    def _():
        st_ref[...] = jnp.zeros_like(st_ref)

    win_ref[0:SSD_HALO, :] = jnp.where(c == 0, 0.0, halo_ref[...])
    win_ref[SSD_HALO:, :] = xbc_ref[...]
    off = SSD_HALO - (SSM_CONV - 1)
    cstep = 512
    for c0 in range(0, d_inner + 2 * gn, cstep):
        acc = jnp.broadcast_to(cb_ref[:, c0:c0 + cstep], (L, cstep))
        for k in range(SSM_CONV):
            acc = acc + win_ref[off + k:off + k + L, c0:c0 + cstep] * cw_ref[k:k + 1, c0:c0 + cstep]
        xc_ref[:, c0:c0 + cstep] = _silu(acc)

    dt = _softplus(dt_ref[...] + dtb_ref[...])
    la = dt * (-jnp.exp(alog_ref[...]))
    ri = lax.broadcasted_iota(jnp.int32, (L, L), 0)
    ci = lax.broadcasted_iota(jnp.int32, (L, L), 1)
    causal = ri >= ci
    tri = jnp.where(causal, 1.0, 0.0).astype(BF16)
    la_h, la_m, la_l = _split3(la)
    cs = _dot(tri, la_h) + _dot(tri, la_m) + _dot(tri, la_l)
    cs_t = cs.T
    dt_t = dt.T
    tw_t = jnp.exp(cs_t[:, L - 1:L] - cs_t) * dt_t
    lane = lax.broadcasted_iota(jnp.int32, (L, LANES), 1)
    left = lane < P

    for g in range(SSM_GROUPS):
        bmat = xc_ref[:, d_inner + g * N:d_inner + (g + 1) * N]
        cmat = xc_ref[:, d_inner + gn + g * N:d_inner + gn + (g + 1) * N]
        cb = _dot_nt(cmat.astype(BF16), bmat.astype(BF16))
        bmat_t = bmat.T
        for pair in range(hpg // 2):
            jp = g * (hpg // 2) + pair
            lanes = slice(jp * LANES, (jp + 1) * LANES)
            xp = xc_ref[:, lanes]
            sp = st_ref[:, lanes]
            y_pair = jnp.zeros((L, LANES), F32)
            upd = jnp.zeros((N, LANES), F32)
            da = jnp.zeros((N, LANES), F32)
            for side in range(2):
                h = 2 * jp + side
                sel = left if side == 0 else jnp.logical_not(left)
                xh = jnp.where(sel, xp, 0.0).astype(BF16)
                sh = jnp.where(sel, sp, 0.0).astype(BF16)
                cs_col = jnp.broadcast_to(cs[:, h:h + 1], (L, L))
                cs_row = cs_t[h:h + 1, :]
                seg = jnp.where(causal, cs_col - cs_row, -jnp.inf)
                mmat = (cb * jnp.exp(seg) * dt_t[h:h + 1, :]).astype(BF16)
                cw = (cmat * jnp.exp(cs_col)).astype(BF16)
                lhs = jnp.concatenate([mmat, cw], axis=1)
                rhs = jnp.concatenate([xh, sh], axis=0)
                y_pair = y_pair + _dot(lhs, rhs)
                bw = (bmat_t * tw_t[h:h + 1, :]).astype(BF16)
                upd = upd + _dot(bw, xh)
                da_h = jnp.exp(cs_t[h:h + 1, L - 1:L])
                da = jnp.where(sel, jnp.broadcast_to(da_h, (N, LANES)), da)
            y_ref[:, lanes] = y_pair
            st_ref[:, lanes] = sp * da + upd

    gw = d_inner // SSM_GROUPS
    for g in range(SSM_GROUPS):
        cols = slice(g * gw, (g + 1) * gw)
        y = (y_ref[:, cols] + dskip_ref[:, cols] * xc_ref[:, cols]) * sz_ref[:, cols]
        ms = jnp.mean(y * y, axis=-1, keepdims=True)
        o_ref[:, cols] = (y * lax.rsqrt(ms + EPS) * nw_ref[:, cols]).astype(o_ref.dtype)


def _ssd(xbc, dt_raw, sz, conv_w, conv_b, dt_bias, a_log, d_skip, norm_w):
    b, s, wx = xbc.shape
    L = SSM_CHUNK
    d_inner = SSM_HEADS * SSM_HEAD_DIM
    hb = L // SSD_HALO
    pad = LANES - SSM_HEADS
    dtb = jnp.pad(dt_bias, (0, pad)).reshape(1, LANES)
    alog = jnp.pad(a_log, (0, pad)).reshape(1, LANES)
    dskip = jnp.repeat(d_skip, SSM_HEAD_DIM).reshape(1, d_inner)
    const = lambda bi, ci: (0, 0)
    return pl.pallas_call(
        _ssd_kernel,
        grid=(b, s // L),
        in_specs=[
            pl.BlockSpec((None, L, wx), lambda bi, ci: (bi, ci, 0)),
            pl.BlockSpec((None, SSD_HALO, wx), lambda bi, ci: (bi, jnp.maximum(ci * hb - 1, 0), 0)),
            pl.BlockSpec((None, L, LANES), lambda bi, ci: (bi, ci, 0)),
            pl.BlockSpec((None, L, d_inner), lambda bi, ci: (bi, ci, 0)),
            pl.BlockSpec((SSM_CONV, wx), const),
            pl.BlockSpec((1, wx), const),
            pl.BlockSpec((1, LANES), const),
            pl.BlockSpec((1, LANES), const),
            pl.BlockSpec((1, d_inner), const),
            pl.BlockSpec((1, d_inner), const),
        ],
        out_specs=pl.BlockSpec((None, L, d_inner), lambda bi, ci: (bi, ci, 0)),
        out_shape=jax.ShapeDtypeStruct((b, s, d_inner), BF16),
        scratch_shapes=[
            pltpu.VMEM((L + SSD_HALO, wx), F32),
            pltpu.VMEM((L, wx), F32),
            pltpu.VMEM((L, d_inner), F32),
            pltpu.VMEM((SSM_STATE, d_inner), F32),
        ],
        compiler_params=_cparams("parallel", "arbitrary"),
        name="ssd_mixer",
    )(xbc, xbc, dt_raw, sz, conv_w, conv_b.reshape(1, wx), dtb, alog, dskip,
      norm_w.reshape(1, d_inner))


def kernel(x, ev_norm_w, ev_w_in, ev_dw_w, ev_dw_b, ev_ln_w, ev_ln_b, ev_w_out, od_norm_w,
           od_w_in, od_conv_w, od_conv_b, od_dt_bias, od_a_log, od_d, od_gnorm_w, od_w_out,
           final_norm_w):
    bsz, seq, d = x.shape
    t = bsz * seq
    x2 = x.reshape(t, d)
    tm = min(1024, t)
    tn = 1024

    cw = ev_dw_w.shape[2]
    aw = ev_w_out.shape[1] - cw
    heads = aw // SB_HEAD_DIM
    w_in = ev_w_in[0]
    scale = SB_HEAD_DIM ** -0.5
    w_glu_a = w_in[:, 0:cw].astype(BF16)
    w_glu_b = w_in[:, cw:2 * cw].astype(BF16)
    w_gates = jnp.concatenate([w_in[:, 2 * cw:3 * cw], w_in[:, 3 * cw + 3 * aw:]], axis=1).astype(BF16)
    w_qkv = jnp.concatenate([w_in[:, 3 * cw:3 * cw + aw] * scale,
                             w_in[:, 3 * cw + aw:3 * cw + 3 * aw]], axis=1).astype(BF16)

    u = _norm_proj_glu(x2, ev_norm_w[0], w_glu_a, w_glu_b, tm=tm, tn=tn, name="proj0_glu")
    qkv = _norm_proj(x2, ev_norm_w[0], w_qkv, act="none", out_dtype=BF16, tm=tm, tn=tn,
                     name="proj0_qkv")
    gates = _norm_proj(x2, ev_norm_w[0], w_gates, act="silu", out_dtype=F32, tm=tm, tn=tn,
                       name="proj0_gates")

    y_conv = _conv_branch(u.reshape(bsz, seq, cw), gates.reshape(bsz, seq, cw + aw),
                          ev_dw_w[0], ev_dw_b[0], ev_ln_w[0], ev_ln_b[0], ts=min(256, seq))
    tq = min(1024, seq)
    y_attn = _attention(qkv.reshape(bsz, seq, 3 * aw), gates.reshape(bsz, seq, cw + aw),
                        heads=heads, tq=tq, tk=min(256, tq))

    w_out = ev_w_out[0].astype(BF16)
    h1 = pl.pallas_call(
        _outproj2_kernel,
        grid=(t // tm,),
        in_specs=[
            pl.BlockSpec((tm, cw), lambda i: (i, 0)),
            pl.BlockSpec((tm, aw), lambda i: (i, 0)),
            pl.BlockSpec((cw, d), lambda i: (0, 0)),
            pl.BlockSpec((aw, d), lambda i: (0, 0)),
            pl.BlockSpec((tm, d), lambda i: (i, 0)),
        ],
        out_specs=pl.BlockSpec((tm, d), lambda i: (i, 0)),
        out_shape=jax.ShapeDtypeStruct((t, d), F32),
        compiler_params=_cparams("parallel"),
        name="outproj0",
    )(y_conv.reshape(t, cw), y_attn.reshape(t, aw), w_out[:cw], w_out[cw:], x2)

    d_inner = SSM_HEADS * SSM_HEAD_DIM
    wx = od_conv_w.shape[2]
    w_in1 = od_w_in[0]
    w_z = w_in1[:, :d_inner].astype(BF16)
    w_xbc = w_in1[:, d_inner:d_inner + wx].astype(BF16)
    w_dt = jnp.pad(w_in1[:, d_inner + wx:], ((0, 0), (0, LANES - SSM_HEADS))).astype(BF16)

    sz = _norm_proj(h1, od_norm_w[0], w_z, act="silu", out_dtype=F32, tm=tm, tn=tn, name="proj1_z")
    xbc = _norm_proj(h1, od_norm_w[0], w_xbc, act="none", out_dtype=F32, tm=tm, tn=tn,
                     name="proj1_xbc")
    dt_raw = _norm_proj(h1, od_norm_w[0], w_dt, act="none", out_dtype=F32, tm=tm, tn=tn,
                        name="proj1_dt")

    yn = _ssd(xbc.reshape(bsz, seq, wx), dt_raw.reshape(bsz, seq, LANES),
              sz.reshape(bsz, seq, d_inner), od_conv_w[0], od_conv_b[0], od_dt_bias[0],
              od_a_log[0], od_d[0], od_gnorm_w[0])

    out = pl.pallas_call(
        _outproj_norm_kernel,
        grid=(t // tm,),
        in_specs=[
            pl.BlockSpec((tm, d_inner), lambda i: (i, 0)),
            pl.BlockSpec((d_inner, d), lambda i: (0, 0)),
            pl.BlockSpec((tm, d), lambda i: (i, 0)),
            pl.BlockSpec((1, d), lambda i: (0, 0)),
        ],
        out_specs=pl.BlockSpec((tm, d), lambda i: (i, 0)),
        out_shape=jax.ShapeDtypeStruct((t, d), F32),
        compiler_params=_cparams("parallel"),
        name="outproj1_norm",
    )(yn.reshape(t, d_inner), od_w_out[0].astype(BF16), h1, final_norm_w.reshape(1, d))
    return out.reshape(bsz, seq, d)
```

```python
import functools

import jax
import jax.numpy as jnp
from jax import lax
from jax.experimental import pallas as pl
from jax.experimental.pallas import tpu as pltpu

F32 = jnp.float32
BF16 = jnp.bfloat16

EPS = 1e-6
LANES = 128
SUBLANES = 8
MXU_WIDTH = 256
LOG2E = 1.4426950408889634
SB_HEAD_DIM = 128
CONF_KERNEL = 31
SSM_HEAD_DIM = 64
SSM_HEADS = 32
SSM_GROUPS = 4
SSM_STATE = 128
SSM_CONV = 4
SSM_CHUNK = 128
VMEM_LIMIT = 56 * 1024 * 1024


def _cparams(*sem):
    return pltpu.CompilerParams(dimension_semantics=sem, vmem_limit_bytes=VMEM_LIMIT)


def _dot(a, b):
    return jnp.dot(a, b, preferred_element_type=F32)


def _dot_nt(a, b):
    return lax.dot_general(a, b, (((1,), (1,)), ((), ())), preferred_element_type=F32)


def _silu(x):
    h = 0.5 * x
    return h * jnp.tanh(h) + h


def _softplus(x):
    return jnp.maximum(x, 0.0) + jnp.log(1.0 + jnp.exp(-jnp.abs(x)))


def _rmsnorm_rows(x_ref, nw_ref, xn_ref, rows, step=256):
    for r in range(0, rows, step):
        x = x_ref[r:r + step, :]
        ms = jnp.mean(x * x, axis=-1, keepdims=True)
        xn_ref[r:r + step, :] = (x * lax.rsqrt(ms + EPS) * nw_ref[...]).astype(BF16)


def _proj_kernel(x_ref, nw_ref, w_ref, o_ref, xn_ref, *, act):
    @pl.when(pl.program_id(1) == 0)
    def _():
        _rmsnorm_rows(x_ref, nw_ref, xn_ref, x_ref.shape[0])

    acc = _dot(xn_ref[...], w_ref[...])
    if act == "silu":
        acc = _silu(acc)
    o_ref[...] = acc.astype(o_ref.dtype)


def _proj_glu_kernel(x_ref, nw_ref, wa_ref, wb_ref, o_ref, xn_ref):
    @pl.when(pl.program_id(1) == 0)
    def _():
        _rmsnorm_rows(x_ref, nw_ref, xn_ref, x_ref.shape[0])

    xn = xn_ref[...]
    a = _dot(xn, wa_ref[...])
    b = _dot(xn, wb_ref[...])
    o_ref[...] = (a * jax.nn.sigmoid(b)).astype(o_ref.dtype)


PROJ_CONV_HALO = 8


def _proj_conv_kernel(x_ref, nw_ref, w_ref, cw_ref, cb_ref, o_ref, xn_ref, win_ref, halo_ref, *,
                      tiles_per_seq):
    i = pl.program_id(0)
    j = pl.program_id(1)
    tm = x_ref.shape[0]

    @pl.when(j == 0)
    def _():
        _rmsnorm_rows(x_ref, nw_ref, xn_ref, tm)

    first = (i % tiles_per_seq) == 0

    @pl.when(first)
    def _():
        win_ref[0:PROJ_CONV_HALO, :] = jnp.zeros((PROJ_CONV_HALO, win_ref.shape[1]), F32)

    @pl.when(jnp.logical_not(first))
    def _():
        win_ref[0:PROJ_CONV_HALO, :] = halo_ref[j]

    off = PROJ_CONV_HALO - (SSM_CONV - 1)
    cstep = MXU_WIDTH
    rstep = 256
    for c0 in range(0, o_ref.shape[1], cstep):
        cols = slice(c0, c0 + cstep)
        win_ref[PROJ_CONV_HALO:, cols] = _dot(xn_ref[...], w_ref[:, cols])
        for r in range(0, tm, rstep):
            acc = jnp.broadcast_to(cb_ref[:, cols], (rstep, cstep))
            for k in range(SSM_CONV):
                acc = acc + win_ref[r + off + k:r + off + k + rstep, cols] * cw_ref[k:k + 1, cols]
            o_ref[r:r + rstep, cols] = acc
    halo_ref[j] = win_ref[tm:tm + PROJ_CONV_HALO, :]


def _norm_proj_conv(x2d, norm_w, w, conv_w, conv_b, *, seq, tm, tn, name):
    t, d = x2d.shape
    n = w.shape[1]
    tn = min(tn, n)
    assert seq % tm == 0
    return pl.pallas_call(
        functools.partial(_proj_conv_kernel, tiles_per_seq=seq // tm),
        grid=(t // tm, n // tn),
        in_specs=[
            pl.BlockSpec((tm, d), lambda i, j: (i, 0)),
            pl.BlockSpec((1, d), lambda i, j: (0, 0)),
            pl.BlockSpec((d, tn), lambda i, j: (0, j)),
            pl.BlockSpec((SSM_CONV, tn), lambda i, j: (0, j)),
            pl.BlockSpec((1, tn), lambda i, j: (0, j)),
        ],
        out_specs=pl.BlockSpec((tm, tn), lambda i, j: (i, j)),
        out_shape=jax.ShapeDtypeStruct((t, n), F32),
        scratch_shapes=[pltpu.VMEM((tm, d), BF16),
                        pltpu.VMEM((tm + PROJ_CONV_HALO, tn), F32),
                        pltpu.VMEM((n // tn, PROJ_CONV_HALO, tn), F32)],
        compiler_params=_cparams("arbitrary", "arbitrary"),
        name=name,
    )(x2d, norm_w.reshape(1, d), w, conv_w, conv_b.reshape(1, n))


def _norm_proj(x2d, norm_w, w, *, act, out_dtype, tm, tn, name):
    t, d = x2d.shape
    n = w.shape[1]
    tn = min(tn, n)
    return pl.pallas_call(
        functools.partial(_proj_kernel, act=act),
        grid=(t // tm, n // tn),
        in_specs=[
            pl.BlockSpec((tm, d), lambda i, j: (i, 0)),
            pl.BlockSpec((1, d), lambda i, j: (0, 0)),
            pl.BlockSpec((d, tn), lambda i, j: (0, j)),
        ],
        out_specs=pl.BlockSpec((tm, tn), lambda i, j: (i, j)),
        out_shape=jax.ShapeDtypeStruct((t, n), out_dtype),
        scratch_shapes=[pltpu.VMEM((tm, d), BF16)],
        compiler_params=_cparams("parallel", "arbitrary"),
        name=name,
    )(x2d, norm_w.reshape(1, d), w)


def _norm_proj_glu(x2d, norm_w, wa, wb, *, tm, tn, name):
    t, d = x2d.shape
    n = wa.shape[1]
    tn = min(tn, n)
    return pl.pallas_call(
        _proj_glu_kernel,
        grid=(t // tm, n // tn),
        in_specs=[
            pl.BlockSpec((tm, d), lambda i, j: (i, 0)),
            pl.BlockSpec((1, d), lambda i, j: (0, 0)),
            pl.BlockSpec((d, tn), lambda i, j: (0, j)),
            pl.BlockSpec((d, tn), lambda i, j: (0, j)),
        ],
        out_specs=pl.BlockSpec((tm, tn), lambda i, j: (i, j)),
        out_shape=jax.ShapeDtypeStruct((t, n), F32),
        scratch_shapes=[pltpu.VMEM((tm, d), BF16)],
        compiler_params=_cparams("parallel", "arbitrary"),
        name=name,
    )(x2d, norm_w.reshape(1, d), wa, wb)


def _outproj2_kernel(ya_ref, yb_ref, wa_ref, wb_ref, res_ref, o_ref):
    acc = _dot(ya_ref[...], wa_ref[...]) + _dot(yb_ref[...], wb_ref[...])
    o_ref[...] = acc + res_ref[...]


def _outproj_norm_kernel(y_ref, w_ref, res_ref, nw_ref, o_ref):
    h = _dot(y_ref[...], w_ref[...]) + res_ref[...]
    ms = jnp.mean(h * h, axis=-1, keepdims=True)
    o_ref[...] = h * lax.rsqrt(ms + EPS) * nw_ref[...]


CONV_HALO = 32
CONV_ROWS = 32


def _conv_branch_kernel(u_ref, halo_ref, g_ref, w_ref, b_ref, lnw_ref, lnb_ref, o_ref, win_ref,
                        sh_ref):
    ts = u_ref.shape[0]
    first = pl.program_id(1) == 0
    win_ref[0:CONV_HALO, :] = jnp.where(first, 0.0, halo_ref[...])
    win_ref[CONV_HALO:, :] = u_ref[...]
    nsh = sh_ref.shape[1]
    for r in range(1, SUBLANES):
        sh_ref[r - 1] = win_ref[r:r + nsh, :]
    off = CONV_HALO - (CONF_KERNEL - 1)
    for r in range(0, ts, CONV_ROWS):
        acc = jnp.broadcast_to(b_ref[...], (CONV_ROWS, u_ref.shape[1]))
        for k in range(CONF_KERNEL):
            res = (off + k) % SUBLANES
            base = r + off + k - res
            if res == 0:
                tap = win_ref[base:base + CONV_ROWS, :]
            else:
                tap = sh_ref[res - 1, base:base + CONV_ROWS, :]
            acc = acc + tap * w_ref[k:k + 1, :]
        mu = jnp.mean(acc, axis=-1, keepdims=True)
        xc = acc - mu
        var = jnp.mean(xc * xc, axis=-1, keepdims=True)
        y = xc * lax.rsqrt(var + EPS) * lnw_ref[...] + lnb_ref[...]
        o_ref[r:r + CONV_ROWS, :] = (_silu(y) * g_ref[r:r + CONV_ROWS, :]).astype(o_ref.dtype)


def _conv_branch(u, gates, dw_w, dw_b, ln_w, ln_b, *, ts):
    b, s, c = u.shape
    hb = ts // CONV_HALO
    w_pad = jnp.zeros((32, c), F32).at[:CONF_KERNEL].set(dw_w)
    return pl.pallas_call(
        _conv_branch_kernel,
        grid=(b, s // ts),
        in_specs=[
            pl.BlockSpec((None, ts, c), lambda bi, i: (bi, i, 0)),
            pl.BlockSpec((None, CONV_HALO, c), lambda bi, i: (bi, jnp.maximum(i * hb - 1, 0), 0)),
            pl.BlockSpec((None, ts, c), lambda bi, i: (bi, i, 0)),
            pl.BlockSpec((32, c), lambda bi, i: (0, 0)),
            pl.BlockSpec((1, c), lambda bi, i: (0, 0)),
            pl.BlockSpec((1, c), lambda bi, i: (0, 0)),
            pl.BlockSpec((1, c), lambda bi, i: (0, 0)),
        ],
        out_specs=pl.BlockSpec((None, ts, c), lambda bi, i: (bi, i, 0)),
        out_shape=jax.ShapeDtypeStruct((b, s, c), BF16),
        scratch_shapes=[pltpu.VMEM((ts + CONV_HALO, c), F32),
                        pltpu.VMEM((SUBLANES - 1, ts + CONV_HALO - SUBLANES, c), F32)],
        compiler_params=_cparams("parallel", "arbitrary"),
        name="conv_branch",
    )(u, u, gates, w_pad, dw_b.reshape(1, c), ln_w.reshape(1, c), ln_b.reshape(1, c))


SKIP_LOG2 = -150.0


def _attn_kernel(q_ref, k_ref, v_ref, g_ref, o_ref, acc_ref, carry_ref, *, tq, tk):
    i = pl.program_id(2)
    acc_ref[...] = jnp.zeros_like(acc_ref)
    carry_ref[...] = jnp.zeros_like(carry_ref)
    rr = lax.broadcasted_iota(jnp.int32, (tk, tk), 0)
    cc = lax.broadcasted_iota(jnp.int32, (tk, tk), 1)
    upper = jnp.where(rr > cc, 1.0, 0.0).astype(BF16)

    def tile(kstart, lo, hi, diag):
        m = hi - lo
        q = q_ref[lo:hi, :]
        k = k_ref[pl.ds(kstart, tk), :]
        v = v_ref[pl.ds(kstart, tk), :]
        z = _dot_nt(q, k)
        sp = jnp.maximum(z, 0.0) + jnp.log(1.0 + jnp.exp2(-jnp.abs(z))) * LOG2E
        lsig = z - sp
        if diag:
            mask = (lax.broadcasted_iota(jnp.int32, (m, tk), 1)
                    < lax.broadcasted_iota(jnp.int32, (m, tk), 0))
            spm = jnp.where(mask, sp, 0.0)
        else:
            spm = sp
        carry = carry_ref[lo:hi, :]
        later = jnp.tile(carry, (1, tk // LANES)) - _dot(spm.astype(BF16), upper)
        w = jnp.exp2(lsig + later)
        if diag:
            w = jnp.where(mask, w, 0.0)
        acc_ref[lo:hi, :] += _dot(w.astype(BF16), v)
        carry_ref[lo:hi, :] = carry - jnp.sum(spm, axis=1, keepdims=True)

    def alive(lo, hi):
        return jnp.max(carry_ref[lo:hi, :]) > SKIP_LOG2

    base = pl.multiple_of(i * tq, tq)
    for lo in range(tq - tk, -1, -tk):
        tile(base + lo, lo, min(lo + 2 * tk, tq), True)
    for lo in range(tq - 3 * tk, -1, -tk):
        @pl.when(alive(lo + 2 * tk, tq))
        def _():
            tile(base + lo, lo + 2 * tk, tq, False)

    def cond(st):
        j, live = st
        return jnp.logical_and(j >= 0, live > 0)

    def body(st):
        j, _ = st
        ks = pl.multiple_of(j * tk, tk)
        top = alive(0, tk)
        rest = alive(tk, tq)

        @pl.when(rest)
        def _():
            tile(ks, 0, tq, False)

        @pl.when(jnp.logical_and(top, jnp.logical_not(rest)))
        def _():
            tile(ks, 0, tk, False)

        return j - 1, jnp.logical_or(top, rest).astype(jnp.int32)

    lax.while_loop(cond, body, (i * (tq // tk) - 1, jnp.int32(1)))
    o_ref[...] = (acc_ref[...] * g_ref[...]).astype(o_ref.dtype)


def _attention(qkv, gates, *, heads, tq, tk):
    b, s, _ = qkv.shape
    d = SB_HEAD_DIM
    assert tq % tk == 0 and tq >= 2 * tk and s % tq == 0 and tk % LANES == 0
    return pl.pallas_call(
        functools.partial(_attn_kernel, tq=tq, tk=tk),
        grid=(b, heads, s // tq),
        in_specs=[
            pl.BlockSpec((None, tq, d), lambda bi, h, i: (bi, i, h)),
            pl.BlockSpec((None, s, d), lambda bi, h, i: (bi, 0, heads + h)),
            pl.BlockSpec((None, s, d), lambda bi, h, i: (bi, 0, 2 * heads + h)),
            pl.BlockSpec((None, tq, d), lambda bi, h, i: (bi, i, heads + h)),
        ],
        out_specs=pl.BlockSpec((None, tq, d), lambda bi, h, i: (bi, i, h)),
        out_shape=jax.ShapeDtypeStruct((b, s, heads * d), BF16),
        scratch_shapes=[
            pltpu.VMEM((tq, d), F32),
            pltpu.VMEM((tq, LANES), F32),
        ],
        compiler_params=_cparams("parallel", "parallel", "arbitrary"),
        name="sb_attention",
    )(qkv, qkv, qkv, gates)


def _split3(x):
    hi = x.astype(BF16)
    r = x - hi.astype(F32)
    mid = r.astype(BF16)
    lo = (r - mid.astype(F32)).astype(BF16)
    return hi, mid, lo


def _ssd_kernel(xbc_ref, dt_ref, sz_ref, dtb_ref, alog_ref, dskip_ref, nw_ref, o_ref, xc_ref, y_ref,
                st_ref):
    L = SSM_CHUNK
    P = SSM_HEAD_DIM
    N = SSM_STATE
    d_inner = SSM_HEADS * P
    gn = SSM_GROUPS * N
    hpg = SSM_HEADS // SSM_GROUPS
    c = pl.program_id(1)

    @pl.when(c == 0)
    def _():
        st_ref[...] = jnp.zeros_like(st_ref)

    cstep = 512
    for c0 in range(0, d_inner + 2 * gn, cstep):
        xc_ref[:, c0:c0 + cstep] = _silu(xbc_ref[:, c0:c0 + cstep])

    dt = _softplus(dt_ref[...] + dtb_ref[...])
    la = dt * (-jnp.exp(alog_ref[...]))
    ri = lax.broadcasted_iota(jnp.int32, (L, L), 0)
    ci = lax.broadcasted_iota(jnp.int32, (L, L), 1)
    causal = ri >= ci
    tri = jnp.where(causal, 1.0, 0.0).astype(BF16)
    la_h, la_m, la_l = _split3(la)
    cs = _dot(tri, la_h) + _dot(tri, la_m) + _dot(tri, la_l)
    cs2 = cs * LOG2E
    cs2_t = cs2.T
    row_t = cs2_t - jnp.log(dt.T) * LOG2E
    tw_t = jnp.exp2(cs2_t[:, L - 1:L] - row_t)
    da_t = jnp.exp2(cs2_t[:, L - 1:L])
    left = lax.broadcasted_iota(jnp.int32, (L, LANES), 1) < P
    left_row = lax.broadcasted_iota(jnp.int32, (1, LANES), 1) < P

    for g in range(SSM_GROUPS):
        bmat = xc_ref[:, d_inner + g * N:d_inner + (g + 1) * N]
        cmat = xc_ref[:, d_inner + gn + g * N:d_inner + gn + (g + 1) * N]
        cb = _dot_nt(cmat.astype(BF16), bmat.astype(BF16))
        bmat_t = bmat.T
        for pair in range(hpg // 2):
            jp = g * (hpg // 2) + pair
            lanes = slice(jp * LANES, (jp + 1) * LANES)
            xp = xc_ref[:, lanes].astype(BF16)
            st_old = st_ref[:, lanes]
            rhs = jnp.concatenate([xp, st_old.astype(BF16)], axis=0)
            ys, upds = [], []
            for h in (2 * jp, 2 * jp + 1):
                col = jnp.broadcast_to(cs2[:, h:h + 1], (L, L))
                seg = jnp.where(causal, col - row_t[h:h + 1, :], -jnp.inf)
                mmat = (cb * jnp.exp2(seg)).astype(BF16)
                cw = (cmat * jnp.exp2(col)).astype(BF16)
                ys.append(_dot(jnp.concatenate([mmat, cw], axis=1), rhs))
                bw = (bmat_t * tw_t[h:h + 1, :]).astype(BF16)
                upds.append(_dot(bw, xp))
            y_ref[:, lanes] = jnp.where(left, ys[0], ys[1])
            da = jnp.where(left_row, da_t[2 * jp:2 * jp + 1, :], da_t[2 * jp + 1:2 * jp + 2, :])
            st_ref[:, lanes] = st_old * da + jnp.where(left, upds[0], upds[1])

    gw = d_inner // SSM_GROUPS
    for g in range(SSM_GROUPS):
        cols = slice(g * gw, (g + 1) * gw)
        y = (y_ref[:, cols] + dskip_ref[:, cols] * xc_ref[:, cols]) * sz_ref[:, cols]
        ms = jnp.mean(y * y, axis=-1, keepdims=True)
        o_ref[:, cols] = (y * lax.rsqrt(ms + EPS) * nw_ref[:, cols]).astype(o_ref.dtype)


def _ssd(xbc, dt_raw, sz, dt_bias, a_log, d_skip, norm_w):
    b, s, wx = xbc.shape
    L = SSM_CHUNK
    d_inner = SSM_HEADS * SSM_HEAD_DIM
    pad = LANES - SSM_HEADS
    dtb = jnp.pad(dt_bias, (0, pad)).reshape(1, LANES)
    alog = jnp.pad(a_log, (0, pad)).reshape(1, LANES)
    dskip = jnp.repeat(d_skip, SSM_HEAD_DIM).reshape(1, d_inner)
    const = lambda bi, ci: (0, 0)
    return pl.pallas_call(
        _ssd_kernel,
        grid=(b, s // L),
        in_specs=[
            pl.BlockSpec((None, L, wx), lambda bi, ci: (bi, ci, 0)),
            pl.BlockSpec((None, L, LANES), lambda bi, ci: (bi, ci, 0)),
            pl.BlockSpec((None, L, d_inner), lambda bi, ci: (bi, ci, 0)),
            pl.BlockSpec((1, LANES), const),
            pl.BlockSpec((1, LANES), const),
            pl.BlockSpec((1, d_inner), const),
            pl.BlockSpec((1, d_inner), const),
        ],
        out_specs=pl.BlockSpec((None, L, d_inner), lambda bi, ci: (bi, ci, 0)),
        out_shape=jax.ShapeDtypeStruct((b, s, d_inner), BF16),
        scratch_shapes=[
            pltpu.VMEM((L, wx), F32),
            pltpu.VMEM((L, d_inner), F32),
            pltpu.VMEM((SSM_STATE, d_inner), F32),
        ],
        compiler_params=_cparams("parallel", "arbitrary"),
        name="ssd_mixer",
    )(xbc, dt_raw, sz, dtb, alog, dskip, norm_w.reshape(1, d_inner))


def kernel(x, ev_norm_w, ev_w_in, ev_dw_w, ev_dw_b, ev_ln_w, ev_ln_b, ev_w_out, od_norm_w,
           od_w_in, od_conv_w, od_conv_b, od_dt_bias, od_a_log, od_d, od_gnorm_w, od_w_out,
           final_norm_w):
    bsz, seq, d = x.shape
    t = bsz * seq
    x2 = x.reshape(t, d)
    tm = min(1024, seq)
    tn = 1024

    cw = ev_dw_w.shape[2]
    aw = ev_w_out.shape[1] - cw
    heads = aw // SB_HEAD_DIM
    w_in = ev_w_in[0]
    scale = SB_HEAD_DIM ** -0.5
    w_glu_a = w_in[:, 0:cw].astype(BF16)
    w_glu_b = w_in[:, cw:2 * cw].astype(BF16)
    w_gates = jnp.concatenate([w_in[:, 2 * cw:3 * cw], w_in[:, 3 * cw + 3 * aw:]], axis=1).astype(BF16)
    w_qkv = jnp.concatenate([w_in[:, 3 * cw:3 * cw + aw] * (scale * LOG2E),
                             w_in[:, 3 * cw + aw:3 * cw + 3 * aw]], axis=1).astype(BF16)

    u = _norm_proj_glu(x2, ev_norm_w[0], w_glu_a, w_glu_b, tm=tm, tn=tn, name="proj0_glu")
    qkv = _norm_proj(x2, ev_norm_w[0], w_qkv, act="none", out_dtype=BF16, tm=tm, tn=tn,
                     name="proj0_qkv")
    gates = _norm_proj(x2, ev_norm_w[0], w_gates, act="silu", out_dtype=F32, tm=tm, tn=tn,
                       name="proj0_gates")

    y_conv = _conv_branch(u.reshape(bsz, seq, cw), gates.reshape(bsz, seq, cw + aw),
                          ev_dw_w[0], ev_dw_b[0], ev_ln_w[0], ev_ln_b[0], ts=min(256, seq))
    tq = min(1024, seq)
    y_attn = _attention(qkv.reshape(bsz, seq, 3 * aw), gates.reshape(bsz, seq, cw + aw),
                        heads=heads, tq=tq, tk=min(256, tq))

    w_out = ev_w_out[0].astype(BF16)
    h1 = pl.pallas_call(
        _outproj2_kernel,
        grid=(t // tm,),
        in_specs=[
            pl.BlockSpec((tm, cw), lambda i: (i, 0)),
            pl.BlockSpec((tm, aw), lambda i: (i, 0)),
            pl.BlockSpec((cw, d), lambda i: (0, 0)),
            pl.BlockSpec((aw, d), lambda i: (0, 0)),
            pl.BlockSpec((tm, d), lambda i: (i, 0)),
        ],
        out_specs=pl.BlockSpec((tm, d), lambda i: (i, 0)),
        out_shape=jax.ShapeDtypeStruct((t, d), F32),
        compiler_params=_cparams("parallel"),
        name="outproj0",
    )(y_conv.reshape(t, cw), y_attn.reshape(t, aw), w_out[:cw], w_out[cw:], x2)

    d_inner = SSM_HEADS * SSM_HEAD_DIM
    wx = od_conv_w.shape[2]
    w_in1 = od_w_in[0]
    w_z = w_in1[:, :d_inner].astype(BF16)
    w_xbc = w_in1[:, d_inner:d_inner + wx].astype(BF16)
    w_dt = jnp.pad(w_in1[:, d_inner + wx:], ((0, 0), (0, LANES - SSM_HEADS))).astype(BF16)

    sz = _norm_proj(h1, od_norm_w[0], w_z, act="silu", out_dtype=F32, tm=tm, tn=tn, name="proj1_z")
    xbc = _norm_proj_conv(h1, od_norm_w[0], w_xbc, od_conv_w[0], od_conv_b[0], seq=seq, tm=tm,
                          tn=tn, name="proj1_xbc_conv")
    dt_raw = _norm_proj(h1, od_norm_w[0], w_dt, act="none", out_dtype=F32, tm=tm, tn=tn,
                        name="proj1_dt")

    yn = _ssd(xbc.reshape(bsz, seq, wx), dt_raw.reshape(bsz, seq, LANES),
              sz.reshape(bsz, seq, d_inner), od_dt_bias[0], od_a_log[0], od_d[0], od_gnorm_w[0])

    out = pl.pallas_call(
        _outproj_norm_kernel,
        grid=(t // tm,),
        in_specs=[
            pl.BlockSpec((tm, d_inner), lambda i: (i, 0)),
            pl.BlockSpec((d_inner, d), lambda i: (0, 0)),
            pl.BlockSpec((tm, d), lambda i: (i, 0)),
            pl.BlockSpec((1, d), lambda i: (0, 0)),
        ],
        out_specs=pl.BlockSpec((tm, d), lambda i: (i, 0)),
        out_shape=jax.ShapeDtypeStruct((t, d), F32),
        compiler_params=_cparams("parallel"),
        name="outproj1_norm",
    )(yn.reshape(t, d_inner), od_w_out[0].astype(BF16), h1, final_norm_w.reshape(1, d))
    return out.reshape(bsz, seq, d)
```

```python
import functools

import jax
import jax.numpy as jnp
from jax import lax
from jax.experimental import pallas as pl
from jax.experimental.pallas import tpu as pltpu

F32 = jnp.float32
BF16 = jnp.bfloat16

EPS = 1e-6
LANES = 128
SUBLANES = 8
LOG2E = 1.4426950408889634
SB_HEAD_DIM = 128
CONF_KERNEL = 31
SSM_HEAD_DIM = 64
SSM_HEADS = 32
SSM_GROUPS = 4
SSM_STATE = 128
SSM_CONV = 4
SSM_CHUNK = 128
VMEM_LIMIT = 56 * 1024 * 1024


def _cparams(*sem):
    return pltpu.CompilerParams(dimension_semantics=sem, vmem_limit_bytes=VMEM_LIMIT)


def _dot(a, b):
    return jnp.dot(a, b, preferred_element_type=F32)


def _dot_nt(a, b):
    return lax.dot_general(a, b, (((1,), (1,)), ((), ())), preferred_element_type=F32)


def _silu(x):
    h = 0.5 * x
    return h * jnp.tanh(h) + h


def _softplus(x):
    return jnp.maximum(x, 0.0) + jnp.log(1.0 + jnp.exp(-jnp.abs(x)))


PROJ_COLS = 1024
PROJ_CONV_HALO = 8


def _rmsnorm_bf16(x_ref, nw_ref, xn_ref):
    rows = x_ref.shape[0]
    step = min(256, rows)
    for r in range(0, rows, step):
        x = x_ref[r:r + step, :]
        ms = jnp.mean(x * x, axis=-1, keepdims=True)
        xn_ref[r:r + step, :] = (x * lax.rsqrt(ms + EPS) * nw_ref[...]).astype(BF16)


def _resident(shape):
    return pl.BlockSpec(shape, lambda i: (0,) * len(shape), pipeline_mode=pl.Buffered(1))


def _proj0_kernel(x_ref, nw_ref, w_ref, u_ref, g_ref, qkv_ref, xn_ref):
    cw = u_ref.shape[1]
    ng = g_ref.shape[1]
    _rmsnorm_bf16(x_ref, nw_ref, xn_ref)
    xn = xn_ref[...]
    for c0 in range(0, cw, PROJ_COLS):
        half = 0.5 * _dot(xn, w_ref[:, c0:c0 + PROJ_COLS])
        gate = _dot(xn, w_ref[:, cw + c0:cw + c0 + PROJ_COLS])
        u_ref[:, c0:c0 + PROJ_COLS] = half * jnp.tanh(0.5 * gate) + half
    for c0 in range(0, ng, PROJ_COLS):
        g_ref[:, c0:c0 + PROJ_COLS] = _silu(_dot(xn, w_ref[:, 2 * cw + c0:2 * cw + c0 + PROJ_COLS]))
    for c0 in range(0, qkv_ref.shape[1], PROJ_COLS):
        lo = 2 * cw + ng + c0
        qkv_ref[:, c0:c0 + PROJ_COLS] = _dot(xn, w_ref[:, lo:lo + PROJ_COLS]).astype(BF16)


def _proj0(x2d, norm_w, w, *, cw, ng, nqkv, tm):
    t, d = x2d.shape
    row = lambda n: pl.BlockSpec((tm, n), lambda i: (i, 0))
    return pl.pallas_call(
        _proj0_kernel,
        grid=(t // tm,),
        in_specs=[row(d), _resident((1, d)), _resident(w.shape)],
        out_specs=[row(cw), row(ng), row(nqkv)],
        out_shape=[jax.ShapeDtypeStruct((t, cw), F32), jax.ShapeDtypeStruct((t, ng), F32),
                   jax.ShapeDtypeStruct((t, nqkv), BF16)],
        scratch_shapes=[pltpu.VMEM((tm, d), BF16)],
        compiler_params=_cparams("parallel"),
        name="proj0",
    )(x2d, norm_w.reshape(1, d), w)


def _proj1_kernel(x_ref, nw_ref, w_ref, cw_ref, cb_ref, sz_ref, xbc_ref, dt_ref, xn_ref, win_ref,
                  raw_ref, *, tiles_per_seq):
    s = pl.program_id(0)
    tm = x_ref.shape[0]
    dz = sz_ref.shape[1]
    wx = xbc_ref.shape[1]

    @pl.when(s == 0)
    def _():
        win_ref[...] = jnp.zeros_like(win_ref)

    off = PROJ_CONV_HALO - (SSM_CONV - 1)
    rstep = min(256, tm)

    for c0 in range(0, wx, PROJ_COLS):
        cols = slice(c0, c0 + PROJ_COLS)
        for r in range(0, tm, rstep):
            acc = jnp.broadcast_to(cb_ref[:, cols], (rstep, PROJ_COLS))
            for k in range(SSM_CONV):
                acc = acc + win_ref[r + off + k:r + off + k + rstep, cols] * cw_ref[k:k + 1, cols]
            xbc_ref[r:r + rstep, cols] = acc

    _rmsnorm_bf16(x_ref, nw_ref, xn_ref)
    xn = xn_ref[...]
    for c0 in range(0, dz, PROJ_COLS):
        sz_ref[:, c0:c0 + PROJ_COLS] = _silu(_dot(xn, w_ref[:, c0:c0 + PROJ_COLS]))
    dt_ref[...] = _dot(xn, w_ref[:, dz + wx:])
    for c0 in range(0, wx, PROJ_COLS):
        raw_ref[:, c0:c0 + PROJ_COLS] = _dot(xn, w_ref[:, dz + c0:dz + c0 + PROJ_COLS])

    first = (s % tiles_per_seq) == 0
    win_ref[0:PROJ_CONV_HALO, :] = jnp.where(first, 0.0, win_ref[tm:tm + PROJ_CONV_HALO, :])
    win_ref[PROJ_CONV_HALO:, :] = raw_ref[...]


def _proj1(x2d, norm_w, w, conv_w, conv_b, *, dz, wx, seq, tm):
    t, d = x2d.shape
    assert seq % tm == 0
    nt = t // tm
    cur = lambda n: pl.BlockSpec((tm, n), lambda s: (jnp.minimum(s, nt - 1), 0))
    prev = lambda n: pl.BlockSpec((tm, n), lambda s: (jnp.maximum(s - 1, 0), 0))
    return pl.pallas_call(
        functools.partial(_proj1_kernel, tiles_per_seq=seq // tm),
        grid=(nt + 1,),
        in_specs=[cur(d), _resident((1, d)), _resident(w.shape), _resident(conv_w.shape),
                  _resident((1, wx))],
        out_specs=[cur(dz), prev(wx), cur(LANES)],
        out_shape=[jax.ShapeDtypeStruct((t, dz), F32), jax.ShapeDtypeStruct((t, wx), F32),
                   jax.ShapeDtypeStruct((t, LANES), F32)],
        scratch_shapes=[pltpu.VMEM((tm, d), BF16),
                        pltpu.VMEM((tm + PROJ_CONV_HALO, wx), F32),
                        pltpu.VMEM((tm, wx), F32)],
        compiler_params=_cparams("arbitrary"),
        name="proj1",
    )(x2d, norm_w.reshape(1, d), w, conv_w, conv_b.reshape(1, wx))


def _outproj2_kernel(ya_ref, yb_ref, wa_ref, wb_ref, res_ref, o_ref):
    acc = _dot(ya_ref[...], wa_ref[...]) + _dot(yb_ref[...], wb_ref[...])
    o_ref[...] = acc + res_ref[...]


def _outproj_norm_kernel(y_ref, w_ref, res_ref, nw_ref, o_ref):
    h = _dot(y_ref[...], w_ref[...]) + res_ref[...]
    ms = jnp.mean(h * h, axis=-1, keepdims=True)
    o_ref[...] = h * lax.rsqrt(ms + EPS) * nw_ref[...]


CONV_HALO = 32
CONV_ROWS = 32


def _conv_branch_kernel(u_ref, halo_ref, g_ref, w_ref, b_ref, lnw_ref, lnb_ref, o_ref, win_ref,
                        sh_ref):
    ts = u_ref.shape[0]
    first = pl.program_id(1) == 0
    win_ref[0:CONV_HALO, :] = jnp.where(first, 0.0, halo_ref[...])
    win_ref[CONV_HALO:, :] = u_ref[...]
    nsh = sh_ref.shape[1]
    for r in range(1, SUBLANES):
        sh_ref[r - 1] = win_ref[r:r + nsh, :]
    off = CONV_HALO - (CONF_KERNEL - 1)
    for r in range(0, ts, CONV_ROWS):
        acc = jnp.broadcast_to(b_ref[...], (CONV_ROWS, u_ref.shape[1]))
        for k in range(CONF_KERNEL):
            res = (off + k) % SUBLANES
            base = r + off + k - res
            if res == 0:
                tap = win_ref[base:base + CONV_ROWS, :]
            else:
                tap = sh_ref[res - 1, base:base + CONV_ROWS, :]
            acc = acc + tap * w_ref[k:k + 1, :]
        mu = jnp.mean(acc, axis=-1, keepdims=True)
        xc = acc - mu
        var = jnp.mean(xc * xc, axis=-1, keepdims=True)
        y = xc * lax.rsqrt(var + EPS) * lnw_ref[...] + lnb_ref[...]
        o_ref[r:r + CONV_ROWS, :] = (_silu(y) * g_ref[r:r + CONV_ROWS, :]).astype(o_ref.dtype)


def _conv_branch(u, gates, dw_w, dw_b, ln_w, ln_b, *, ts):
    b, s, c = u.shape
    hb = ts // CONV_HALO
    w_pad = jnp.zeros((32, c), F32).at[:CONF_KERNEL].set(dw_w)
    return pl.pallas_call(
        _conv_branch_kernel,
        grid=(b, s // ts),
        in_specs=[
            pl.BlockSpec((None, ts, c), lambda bi, i: (bi, i, 0)),
            pl.BlockSpec((None, CONV_HALO, c), lambda bi, i: (bi, jnp.maximum(i * hb - 1, 0), 0)),
            pl.BlockSpec((None, ts, c), lambda bi, i: (bi, i, 0)),
            pl.BlockSpec((32, c), lambda bi, i: (0, 0)),
            pl.BlockSpec((1, c), lambda bi, i: (0, 0)),
            pl.BlockSpec((1, c), lambda bi, i: (0, 0)),
            pl.BlockSpec((1, c), lambda bi, i: (0, 0)),
        ],
        out_specs=pl.BlockSpec((None, ts, c), lambda bi, i: (bi, i, 0)),
        out_shape=jax.ShapeDtypeStruct((b, s, c), BF16),
        scratch_shapes=[pltpu.VMEM((ts + CONV_HALO, c), F32),
                        pltpu.VMEM((SUBLANES - 1, ts + CONV_HALO - SUBLANES, c), F32)],
        compiler_params=_cparams("parallel", "arbitrary"),
        name="conv_branch",
    )(u, u, gates, w_pad, dw_b.reshape(1, c), ln_w.reshape(1, c), ln_b.reshape(1, c))


SKIP_LOG2 = -150.0


def _attn_kernel(q_ref, k_ref, v_ref, g_ref, o_ref, acc_ref, carry_ref, *, tq, tk):
    i = pl.program_id(2)
    acc_ref[...] = jnp.zeros_like(acc_ref)
    carry_ref[...] = jnp.zeros_like(carry_ref)
    rr = lax.broadcasted_iota(jnp.int32, (tk, tk), 0)
    cc = lax.broadcasted_iota(jnp.int32, (tk, tk), 1)
    upper = jnp.where(rr > cc, 1.0, 0.0).astype(BF16)

    def tile(kstart, lo, hi, diag):
        m = hi - lo
        q = q_ref[lo:hi, :]
        k = k_ref[pl.ds(kstart, tk), :]
        v = v_ref[pl.ds(kstart, tk), :]
        z = _dot_nt(q, k)
        sp = jnp.maximum(z, 0.0) + jnp.log(1.0 + jnp.exp2(-jnp.abs(z))) * LOG2E
        lsig = z - sp
        if diag:
            mask = (lax.broadcasted_iota(jnp.int32, (m, tk), 1)
                    < lax.broadcasted_iota(jnp.int32, (m, tk), 0))
            spm = jnp.where(mask, sp, 0.0)
        else:
            spm = sp
        carry = carry_ref[lo:hi, :]
        later = jnp.tile(carry, (1, tk // LANES)) - _dot(spm.astype(BF16), upper)
        w = jnp.exp2(lsig + later)
        if diag:
            w = jnp.where(mask, w, 0.0)
        acc_ref[lo:hi, :] += _dot(w.astype(BF16), v)
        carry_ref[lo:hi, :] = carry - jnp.sum(spm, axis=1, keepdims=True)

    def alive(lo, hi):
        return jnp.max(carry_ref[lo:hi, :]) > SKIP_LOG2

    base = pl.multiple_of(i * tq, tq)
    for lo in range(tq - tk, -1, -tk):
        tile(base + lo, lo, min(lo + 2 * tk, tq), True)
    for lo in range(tq - 3 * tk, -1, -tk):
        @pl.when(alive(lo + 2 * tk, tq))
        def _():
            tile(base + lo, lo + 2 * tk, tq, False)

    def cond(st):
        j, live = st
        return jnp.logical_and(j >= 0, live > 0)

    def body(st):
        j, _ = st
        ks = pl.multiple_of(j * tk, tk)
        top = alive(0, tk)
        rest = alive(tk, tq)

        @pl.when(rest)
        def _():
            tile(ks, 0, tq, False)

        @pl.when(jnp.logical_and(top, jnp.logical_not(rest)))
        def _():
            tile(ks, 0, tk, False)

        return j - 1, jnp.logical_or(top, rest).astype(jnp.int32)

    lax.while_loop(cond, body, (i * (tq // tk) - 1, jnp.int32(1)))
    o_ref[...] = (acc_ref[...] * g_ref[...]).astype(o_ref.dtype)


def _attention(qkv, gates, *, heads, tq, tk):
    b, s, _ = qkv.shape
    d = SB_HEAD_DIM
    assert tq % tk == 0 and tq >= 2 * tk and s % tq == 0 and tk % LANES == 0
    return pl.pallas_call(
        functools.partial(_attn_kernel, tq=tq, tk=tk),
        grid=(b, heads, s // tq),
        in_specs=[
            pl.BlockSpec((None, tq, d), lambda bi, h, i: (bi, i, h)),
            pl.BlockSpec((None, s, d), lambda bi, h, i: (bi, 0, heads + h)),
            pl.BlockSpec((None, s, d), lambda bi, h, i: (bi, 0, 2 * heads + h)),
            pl.BlockSpec((None, tq, d), lambda bi, h, i: (bi, i, heads + h)),
        ],
        out_specs=pl.BlockSpec((None, tq, d), lambda bi, h, i: (bi, i, h)),
        out_shape=jax.ShapeDtypeStruct((b, s, heads * d), BF16),
        scratch_shapes=[
            pltpu.VMEM((tq, d), F32),
            pltpu.VMEM((tq, LANES), F32),
        ],
        compiler_params=_cparams("parallel", "parallel", "arbitrary"),
        name="sb_attention",
    )(qkv, qkv, qkv, gates)


def _split3(x):
    hi = x.astype(BF16)
    r = x - hi.astype(F32)
    mid = r.astype(BF16)
    lo = (r - mid.astype(F32)).astype(BF16)
    return hi, mid, lo


def _ssd_kernel(xbc_ref, dt_ref, sz_ref, dtb_ref, alog_ref, dskip_ref, nw_ref, o_ref, xc_ref, y_ref,
                st_ref):
    L = SSM_CHUNK
    P = SSM_HEAD_DIM
    N = SSM_STATE
    d_inner = SSM_HEADS * P
    gn = SSM_GROUPS * N
    hpg = SSM_HEADS // SSM_GROUPS
    c = pl.program_id(1)

    @pl.when(c == 0)
    def _():
        st_ref[...] = jnp.zeros_like(st_ref)

    cstep = 512
    for c0 in range(0, d_inner + 2 * gn, cstep):
        xc_ref[:, c0:c0 + cstep] = _silu(xbc_ref[:, c0:c0 + cstep])

    dt = _softplus(dt_ref[...] + dtb_ref[...])
    la = dt * (-jnp.exp(alog_ref[...]))
    ri = lax.broadcasted_iota(jnp.int32, (L, L), 0)
    ci = lax.broadcasted_iota(jnp.int32, (L, L), 1)
    causal = ri >= ci
    tri = jnp.where(causal, 1.0, 0.0).astype(BF16)
    la_h, la_m, la_l = _split3(la)
    cs = _dot(tri, la_h) + _dot(tri, la_m) + _dot(tri, la_l)
    cs2 = cs * LOG2E
    cs2_t = cs2.T
    row_t = cs2_t - jnp.log(dt.T) * LOG2E
    tw_t = jnp.exp2(cs2_t[:, L - 1:L] - row_t)
    da_t = jnp.exp2(cs2_t[:, L - 1:L])
    left = lax.broadcasted_iota(jnp.int32, (L, LANES), 1) < P
    left_row = lax.broadcasted_iota(jnp.int32, (1, LANES), 1) < P

    for g in range(SSM_GROUPS):
        bmat = xc_ref[:, d_inner + g * N:d_inner + (g + 1) * N]
        cmat = xc_ref[:, d_inner + gn + g * N:d_inner + gn + (g + 1) * N]
        cb = _dot_nt(cmat.astype(BF16), bmat.astype(BF16))
        bmat_t = bmat.T
        for pair in range(hpg // 2):
            jp = g * (hpg // 2) + pair
            lanes = slice(jp * LANES, (jp + 1) * LANES)
            xp = xc_ref[:, lanes].astype(BF16)
            st_old = st_ref[:, lanes]
            rhs = jnp.concatenate([xp, st_old.astype(BF16)], axis=0)
            ys, upds = [], []
            for h in (2 * jp, 2 * jp + 1):
                col = jnp.broadcast_to(cs2[:, h:h + 1], (L, L))
                seg = jnp.where(causal, col - row_t[h:h + 1, :], -jnp.inf)
                mmat = (cb * jnp.exp2(seg)).astype(BF16)
                cw = (cmat * jnp.exp2(col)).astype(BF16)
                ys.append(_dot(jnp.concatenate([mmat, cw], axis=1), rhs))
                bw = (bmat_t * tw_t[h:h + 1, :]).astype(BF16)
                upds.append(_dot(bw, xp))
            y_ref[:, lanes] = jnp.where(left, ys[0], ys[1])
            da = jnp.where(left_row, da_t[2 * jp:2 * jp + 1, :], da_t[2 * jp + 1:2 * jp + 2, :])
            st_ref[:, lanes] = st_old * da + jnp.where(left, upds[0], upds[1])

    gw = d_inner // SSM_GROUPS
    for g in range(SSM_GROUPS):
        cols = slice(g * gw, (g + 1) * gw)
        y = (y_ref[:, cols] + dskip_ref[:, cols] * xc_ref[:, cols]) * sz_ref[:, cols]
        ms = jnp.mean(y * y, axis=-1, keepdims=True)
        o_ref[:, cols] = (y * lax.rsqrt(ms + EPS) * nw_ref[:, cols]).astype(o_ref.dtype)


def _ssd(xbc, dt_raw, sz, dt_bias, a_log, d_skip, norm_w):
    b, s, wx = xbc.shape
    L = SSM_CHUNK
    d_inner = SSM_HEADS * SSM_HEAD_DIM
    pad = LANES - SSM_HEADS
    dtb = jnp.pad(dt_bias, (0, pad)).reshape(1, LANES)
    alog = jnp.pad(a_log, (0, pad)).reshape(1, LANES)
    dskip = jnp.repeat(d_skip, SSM_HEAD_DIM).reshape(1, d_inner)
    const = lambda bi, ci: (0, 0)
    return pl.pallas_call(
        _ssd_kernel,
        grid=(b, s // L),
        in_specs=[
            pl.BlockSpec((None, L, wx), lambda bi, ci: (bi, ci, 0)),
            pl.BlockSpec((None, L, LANES), lambda bi, ci: (bi, ci, 0)),
            pl.BlockSpec((None, L, d_inner), lambda bi, ci: (bi, ci, 0)),
            pl.BlockSpec((1, LANES), const),
            pl.BlockSpec((1, LANES), const),
            pl.BlockSpec((1, d_inner), const),
            pl.BlockSpec((1, d_inner), const),
        ],
        out_specs=pl.BlockSpec((None, L, d_inner), lambda bi, ci: (bi, ci, 0)),
        out_shape=jax.ShapeDtypeStruct((b, s, d_inner), BF16),
        scratch_shapes=[
            pltpu.VMEM((L, wx), F32),
            pltpu.VMEM((L, d_inner), F32),
            pltpu.VMEM((SSM_STATE, d_inner), F32),
        ],
        compiler_params=_cparams("parallel", "arbitrary"),
        name="ssd_mixer",
    )(xbc, dt_raw, sz, dtb, alog, dskip, norm_w.reshape(1, d_inner))


def kernel(x, ev_norm_w, ev_w_in, ev_dw_w, ev_dw_b, ev_ln_w, ev_ln_b, ev_w_out, od_norm_w,
           od_w_in, od_conv_w, od_conv_b, od_dt_bias, od_a_log, od_d, od_gnorm_w, od_w_out,
           final_norm_w):
    bsz, seq, d = x.shape
    t = bsz * seq
    x2 = x.reshape(t, d)
    tm = min(1024, seq)
    tp = min(512, seq)

    cw = ev_dw_w.shape[2]
    aw = ev_w_out.shape[1] - cw
    heads = aw // SB_HEAD_DIM
    w_in = ev_w_in[0]
    scale = SB_HEAD_DIM ** -0.5
    w0 = jnp.concatenate([w_in[:, :3 * cw], w_in[:, 3 * cw + 3 * aw:],
                          w_in[:, 3 * cw:3 * cw + aw] * (scale * LOG2E),
                          w_in[:, 3 * cw + aw:3 * cw + 3 * aw]], axis=1).astype(BF16)
    u, gates, qkv = _proj0(x2, ev_norm_w[0], w0, cw=cw, ng=cw + aw, nqkv=3 * aw, tm=tp)

    y_conv = _conv_branch(u.reshape(bsz, seq, cw), gates.reshape(bsz, seq, cw + aw),
                          ev_dw_w[0], ev_dw_b[0], ev_ln_w[0], ev_ln_b[0], ts=min(256, seq))
    tq = min(1024, seq)
    y_attn = _attention(qkv.reshape(bsz, seq, 3 * aw), gates.reshape(bsz, seq, cw + aw),
                        heads=heads, tq=tq, tk=min(256, tq))

    w_out = ev_w_out[0].astype(BF16)
    h1 = pl.pallas_call(
        _outproj2_kernel,
        grid=(t // tm,),
        in_specs=[
            pl.BlockSpec((tm, cw), lambda i: (i, 0)),
            pl.BlockSpec((tm, aw), lambda i: (i, 0)),
            pl.BlockSpec((cw, d), lambda i: (0, 0)),
            pl.BlockSpec((aw, d), lambda i: (0, 0)),
            pl.BlockSpec((tm, d), lambda i: (i, 0)),
        ],
        out_specs=pl.BlockSpec((tm, d), lambda i: (i, 0)),
        out_shape=jax.ShapeDtypeStruct((t, d), F32),
        compiler_params=_cparams("parallel"),
        name="outproj0",
    )(y_conv.reshape(t, cw), y_attn.reshape(t, aw), w_out[:cw], w_out[cw:], x2)

    d_inner = SSM_HEADS * SSM_HEAD_DIM
    wx = od_conv_w.shape[2]
    w1 = jnp.pad(od_w_in[0], ((0, 0), (0, LANES - SSM_HEADS))).astype(BF16)
    sz, xbc, dt_raw = _proj1(h1, od_norm_w[0], w1, od_conv_w[0], od_conv_b[0], dz=d_inner, wx=wx,
                             seq=seq, tm=tp)

    yn = _ssd(xbc.reshape(bsz, seq, wx), dt_raw.reshape(bsz, seq, LANES),
              sz.reshape(bsz, seq, d_inner), od_dt_bias[0], od_a_log[0], od_d[0], od_gnorm_w[0])

    out = pl.pallas_call(
        _outproj_norm_kernel,
        grid=(t // tm,),
        in_specs=[
            pl.BlockSpec((tm, d_inner), lambda i: (i, 0)),
            pl.BlockSpec((d_inner, d), lambda i: (0, 0)),
            pl.BlockSpec((tm, d), lambda i: (i, 0)),
            pl.BlockSpec((1, d), lambda i: (0, 0)),
        ],
        out_specs=pl.BlockSpec((tm, d), lambda i: (i, 0)),
        out_shape=jax.ShapeDtypeStruct((t, d), F32),
        compiler_params=_cparams("parallel"),
        name="outproj1_norm",
    )(yn.reshape(t, d_inner), od_w_out[0].astype(BF16), h1, final_norm_w.reshape(1, d))
    return out.reshape(bsz, seq, d)
```

```python
import functools

import jax
import jax.numpy as jnp
from jax import lax
from jax.experimental import pallas as pl
from jax.experimental.pallas import tpu as pltpu

F32 = jnp.float32
BF16 = jnp.bfloat16

EPS = 1e-6
LANES = 128
SUBLANES = 8
LOG2E = 1.4426950408889634
SB_HEAD_DIM = 128
CONF_KERNEL = 31
SSM_HEAD_DIM = 64
SSM_HEADS = 32
SSM_GROUPS = 4
SSM_STATE = 128
SSM_CONV = 4
SSM_CHUNK = 128
VMEM_LIMIT = 56 * 1024 * 1024


def _cparams(*sem):
    return pltpu.CompilerParams(dimension_semantics=sem, vmem_limit_bytes=VMEM_LIMIT)


def _dot(a, b):
    return jnp.dot(a, b, preferred_element_type=F32)


def _dot_nt(a, b):
    return lax.dot_general(a, b, (((1,), (1,)), ((), ())), preferred_element_type=F32)


def _silu(x):
    h = 0.5 * x
    return h * jnp.tanh(h) + h


def _softplus(x):
    return jnp.maximum(x, 0.0) + jnp.log(1.0 + jnp.exp(-jnp.abs(x)))


PROJ_COLS = 1024
PROJ_CONV_HALO = 8


def _rmsnorm_bf16(x_ref, nw_ref, xn_ref):
    rows = x_ref.shape[0]
    step = min(256, rows)
    for r in range(0, rows, step):
        x = x_ref[r:r + step, :]
        ms = jnp.mean(x * x, axis=-1, keepdims=True)
        xn_ref[r:r + step, :] = (x * lax.rsqrt(ms + EPS) * nw_ref[...]).astype(BF16)


def _resident(shape):
    return pl.BlockSpec(shape, lambda i: (0,) * len(shape), pipeline_mode=pl.Buffered(1))


def _proj0_kernel(x_ref, nw_ref, w_ref, u_ref, g_ref, qkv_ref, xn_ref):
    cw = u_ref.shape[1]
    ng = g_ref.shape[1]
    _rmsnorm_bf16(x_ref, nw_ref, xn_ref)
    xn = xn_ref[...]
    for c0 in range(0, cw, PROJ_COLS):
        half = 0.5 * _dot(xn, w_ref[:, c0:c0 + PROJ_COLS])
        gate = _dot(xn, w_ref[:, cw + c0:cw + c0 + PROJ_COLS])
        u_ref[:, c0:c0 + PROJ_COLS] = half * jnp.tanh(0.5 * gate) + half
    for c0 in range(0, ng, PROJ_COLS):
        g_ref[:, c0:c0 + PROJ_COLS] = _silu(_dot(xn, w_ref[:, 2 * cw + c0:2 * cw + c0 + PROJ_COLS]))
    for c0 in range(0, qkv_ref.shape[1], PROJ_COLS):
        lo = 2 * cw + ng + c0
        qkv_ref[:, c0:c0 + PROJ_COLS] = _dot(xn, w_ref[:, lo:lo + PROJ_COLS]).astype(BF16)


def _proj0(x2d, norm_w, w, *, cw, ng, nqkv, tm):
    t, d = x2d.shape
    row = lambda n: pl.BlockSpec((tm, n), lambda i: (i, 0))
    return pl.pallas_call(
        _proj0_kernel,
        grid=(t // tm,),
        in_specs=[row(d), _resident((1, d)), _resident(w.shape)],
        out_specs=[row(cw), row(ng), row(nqkv)],
        out_shape=[jax.ShapeDtypeStruct((t, cw), F32), jax.ShapeDtypeStruct((t, ng), F32),
                   jax.ShapeDtypeStruct((t, nqkv), BF16)],
        scratch_shapes=[pltpu.VMEM((tm, d), BF16)],
        compiler_params=_cparams("parallel"),
        name="proj0",
    )(x2d, norm_w.reshape(1, d), w)


def _proj1_kernel(x_ref, nw_ref, w_ref, cw_ref, cb_ref, sz_ref, xbc_ref, dt_ref, xn_ref, win_ref,
                  raw_ref, *, tiles_per_seq):
    s = pl.program_id(0)
    tm = x_ref.shape[0]
    dz = sz_ref.shape[1]
    wx = xbc_ref.shape[1]

    @pl.when(s == 0)
    def _():
        win_ref[...] = jnp.zeros_like(win_ref)

    off = PROJ_CONV_HALO - (SSM_CONV - 1)
    rstep = min(256, tm)

    for c0 in range(0, wx, PROJ_COLS):
        cols = slice(c0, c0 + PROJ_COLS)
        for r in range(0, tm, rstep):
            acc = jnp.broadcast_to(cb_ref[:, cols], (rstep, PROJ_COLS))
            for k in range(SSM_CONV):
                wk = cw_ref[k * SUBLANES:(k + 1) * SUBLANES, cols]
                acc = acc + (win_ref[r + off + k:r + off + k + rstep, cols]
                             * jnp.concatenate([wk] * (rstep // SUBLANES), axis=0))
            xbc_ref[r:r + rstep, cols] = acc

    _rmsnorm_bf16(x_ref, nw_ref, xn_ref)
    xn = xn_ref[...]
    for c0 in range(0, dz, PROJ_COLS):
        sz_ref[:, c0:c0 + PROJ_COLS] = _silu(_dot(xn, w_ref[:, c0:c0 + PROJ_COLS]))
    dt_ref[...] = _dot(xn, w_ref[:, dz + wx:])
    for c0 in range(0, wx, PROJ_COLS):
        raw_ref[:, c0:c0 + PROJ_COLS] = _dot(xn, w_ref[:, dz + c0:dz + c0 + PROJ_COLS])

    first = (s % tiles_per_seq) == 0
    win_ref[0:PROJ_CONV_HALO, :] = jnp.where(first, 0.0, win_ref[tm:tm + PROJ_CONV_HALO, :])
    win_ref[PROJ_CONV_HALO:, :] = raw_ref[...]


def _proj1(x2d, norm_w, w, conv_w, conv_b, *, dz, wx, seq, tm):
    t, d = x2d.shape
    assert seq % tm == 0
    nt = t // tm
    cur = lambda n: pl.BlockSpec((tm, n), lambda s: (jnp.minimum(s, nt - 1), 0))
    prev = lambda n: pl.BlockSpec((tm, n), lambda s: (jnp.maximum(s - 1, 0), 0))
    cw_rep = jnp.repeat(conv_w, SUBLANES, axis=0)
    return pl.pallas_call(
        functools.partial(_proj1_kernel, tiles_per_seq=seq // tm),
        grid=(nt + 1,),
        in_specs=[cur(d), _resident((1, d)), _resident(w.shape), _resident(cw_rep.shape),
                  _resident((1, wx))],
        out_specs=[cur(dz), prev(wx), cur(LANES)],
        out_shape=[jax.ShapeDtypeStruct((t, dz), F32), jax.ShapeDtypeStruct((t, wx), F32),
                   jax.ShapeDtypeStruct((t, LANES), F32)],
        scratch_shapes=[pltpu.VMEM((tm, d), BF16),
                        pltpu.VMEM((tm + PROJ_CONV_HALO, wx), F32),
                        pltpu.VMEM((tm, wx), F32)],
        compiler_params=_cparams("arbitrary"),
        name="proj1",
    )(x2d, norm_w.reshape(1, d), w, cw_rep, conv_b.reshape(1, wx))


def _outproj2_kernel(ya_ref, yb_ref, wa_ref, wb_ref, res_ref, o_ref):
    acc = _dot(ya_ref[...], wa_ref[...]) + _dot(yb_ref[...], wb_ref[...])
    o_ref[...] = acc + res_ref[...]


def _outproj_norm_kernel(y_ref, w_ref, res_ref, nw_ref, o_ref):
    h = _dot(y_ref[...], w_ref[...]) + res_ref[...]
    ms = jnp.mean(h * h, axis=-1, keepdims=True)
    o_ref[...] = h * lax.rsqrt(ms + EPS) * nw_ref[...]


CONV_HALO = 32
CONV_ROWS = 32


def _conv_branch_kernel(u_ref, halo_ref, g_ref, w_ref, b_ref, lnw_ref, lnb_ref, o_ref, win_ref,
                        sh_ref):
    ts = u_ref.shape[0]
    first = pl.program_id(1) == 0
    win_ref[0:CONV_HALO, :] = jnp.where(first, 0.0, halo_ref[...])
    win_ref[CONV_HALO:, :] = u_ref[...]
    nsh = sh_ref.shape[1]
    for r in range(1, SUBLANES):
        sh_ref[r - 1] = win_ref[r:r + nsh, :]
    off = CONV_HALO - (CONF_KERNEL - 1)
    for r in range(0, ts, CONV_ROWS):
        acc = jnp.broadcast_to(b_ref[...], (CONV_ROWS, u_ref.shape[1]))
        for k in range(CONF_KERNEL):
            res = (off + k) % SUBLANES
            base = r + off + k - res
            if res == 0:
                tap = win_ref[base:base + CONV_ROWS, :]
            else:
                tap = sh_ref[res - 1, base:base + CONV_ROWS, :]
            wk = w_ref[k * SUBLANES:(k + 1) * SUBLANES, :]
            acc = acc + tap * jnp.concatenate([wk] * (CONV_ROWS // SUBLANES), axis=0)
        mu = jnp.mean(acc, axis=-1, keepdims=True)
        xc = acc - mu
        var = jnp.mean(xc * xc, axis=-1, keepdims=True)
        y = xc * lax.rsqrt(var + EPS) * lnw_ref[...] + lnb_ref[...]
        o_ref[r:r + CONV_ROWS, :] = (_silu(y) * g_ref[r:r + CONV_ROWS, :]).astype(o_ref.dtype)


def _conv_branch(u, gates, dw_w, dw_b, ln_w, ln_b, *, ts):
    b, s, c = u.shape
    hb = ts // CONV_HALO
    w_rep = jnp.repeat(dw_w, SUBLANES, axis=0)
    return pl.pallas_call(
        _conv_branch_kernel,
        grid=(b, s // ts),
        in_specs=[
            pl.BlockSpec((None, ts, c), lambda bi, i: (bi, i, 0)),
            pl.BlockSpec((None, CONV_HALO, c), lambda bi, i: (bi, jnp.maximum(i * hb - 1, 0), 0)),
            pl.BlockSpec((None, ts, c), lambda bi, i: (bi, i, 0)),
            pl.BlockSpec((CONF_KERNEL * SUBLANES, c), lambda bi, i: (0, 0)),
            pl.BlockSpec((1, c), lambda bi, i: (0, 0)),
            pl.BlockSpec((1, c), lambda bi, i: (0, 0)),
            pl.BlockSpec((1, c), lambda bi, i: (0, 0)),
        ],
        out_specs=pl.BlockSpec((None, ts, c), lambda bi, i: (bi, i, 0)),
        out_shape=jax.ShapeDtypeStruct((b, s, c), BF16),
        scratch_shapes=[pltpu.VMEM((ts + CONV_HALO, c), F32),
                        pltpu.VMEM((SUBLANES - 1, ts + CONV_HALO - SUBLANES, c), F32)],
        compiler_params=_cparams("parallel", "arbitrary"),
        name="conv_branch",
    )(u, u, gates, w_rep, dw_b.reshape(1, c), ln_w.reshape(1, c), ln_b.reshape(1, c))


SKIP_LOG2 = -150.0


def _attn_kernel(q_ref, k_ref, v_ref, g_ref, o_ref, acc_ref, carry_ref, *, tq, tk, nh):
    i = pl.program_id(2)
    d = SB_HEAD_DIM
    acc_ref[...] = jnp.zeros_like(acc_ref)
    carry_ref[...] = jnp.zeros_like(carry_ref)
    rr = lax.broadcasted_iota(jnp.int32, (tk, tk), 0)
    cc = lax.broadcasted_iota(jnp.int32, (tk, tk), 1)
    upper = jnp.where(rr > cc, 1.0, 0.0).astype(BF16)

    def tile(hh, kstart, lo, hi, diag):
        m = hi - lo
        hl = slice(hh * d, (hh + 1) * d)
        q = q_ref[lo:hi, hl]
        k = k_ref[pl.ds(kstart, tk), hl]
        v = v_ref[pl.ds(kstart, tk), hl]
        z = _dot_nt(q, k)
        sp = jnp.maximum(z, 0.0) + jnp.log(1.0 + jnp.exp2(-jnp.abs(z))) * LOG2E
        lsig = z - sp
        if diag:
            mask = (lax.broadcasted_iota(jnp.int32, (m, tk), 1)
                    < lax.broadcasted_iota(jnp.int32, (m, tk), 0))
            spm = jnp.where(mask, sp, 0.0)
        else:
            spm = sp
        carry = carry_ref[hh, lo:hi, :]
        later = jnp.tile(carry, (1, tk // LANES)) - _dot(spm.astype(BF16), upper)
        w = jnp.exp2(lsig + later)
        if diag:
            w = jnp.where(mask, w, 0.0)
        acc_ref[hh, lo:hi, :] += _dot(w.astype(BF16), v)
        carry_ref[hh, lo:hi, :] = carry - jnp.sum(spm, axis=1, keepdims=True)

    def alive(hh, lo, hi):
        return jnp.max(carry_ref[hh, lo:hi, :]) > SKIP_LOG2

    base = pl.multiple_of(i * tq, tq)
    for lo in range(tq - tk, -1, -tk):
        for hh in range(nh):
            tile(hh, base + lo, lo, min(lo + 2 * tk, tq), True)
    for hh in range(nh):
        for lo in range(tq - 3 * tk, -1, -tk):
            @pl.when(alive(hh, lo + 2 * tk, tq))
            def _():
                tile(hh, base + lo, lo + 2 * tk, tq, False)

    def cond(st):
        j, live = st
        return jnp.logical_and(j >= 0, live > 0)

    for hh in range(nh):
        def body(st, hh=hh):
            j, _ = st
            ks = pl.multiple_of(j * tk, tk)
            top = alive(hh, 0, tk)
            rest = alive(hh, tk, tq)

            @pl.when(rest)
            def _():
                tile(hh, ks, 0, tq, False)

            @pl.when(jnp.logical_and(top, jnp.logical_not(rest)))
            def _():
                tile(hh, ks, 0, tk, False)

            return j - 1, jnp.logical_or(top, rest).astype(jnp.int32)

        lax.while_loop(cond, body, (i * (tq // tk) - 1, jnp.int32(1)))
        hl = slice(hh * d, (hh + 1) * d)
        o_ref[:, hl] = (acc_ref[hh] * g_ref[:, hl]).astype(o_ref.dtype)


ATTN_HEADS_PER_STEP = 2


def _attention(qkv, gates, *, heads, tq, tk):
    b, s, _ = qkv.shape
    nh = ATTN_HEADS_PER_STEP
    dd = nh * SB_HEAD_DIM
    hb = heads // nh
    gate_off = (gates.shape[2] - heads * SB_HEAD_DIM) // dd
    assert tq % tk == 0 and tq >= 2 * tk and s % tq == 0 and tk % LANES == 0 and heads % nh == 0
    return pl.pallas_call(
        functools.partial(_attn_kernel, tq=tq, tk=tk, nh=nh),
        grid=(b, hb, s // tq),
        in_specs=[
            pl.BlockSpec((None, tq, dd), lambda bi, h, i: (bi, i, h)),
            pl.BlockSpec((None, s, dd), lambda bi, h, i: (bi, 0, hb + h)),
            pl.BlockSpec((None, s, dd), lambda bi, h, i: (bi, 0, 2 * hb + h)),
            pl.BlockSpec((None, tq, dd), lambda bi, h, i: (bi, i, gate_off + h)),
        ],
        out_specs=pl.BlockSpec((None, tq, dd), lambda bi, h, i: (bi, i, h)),
        out_shape=jax.ShapeDtypeStruct((b, s, heads * SB_HEAD_DIM), BF16),
        scratch_shapes=[
            pltpu.VMEM((nh, tq, SB_HEAD_DIM), F32),
            pltpu.VMEM((nh, tq, LANES), F32),
        ],
        compiler_params=_cparams("parallel", "parallel", "arbitrary"),
        name="sb_attention",
    )(qkv, qkv, qkv, gates)


def _split3(x):
    hi = x.astype(BF16)
    r = x - hi.astype(F32)
    mid = r.astype(BF16)
    lo = (r - mid.astype(F32)).astype(BF16)
    return hi, mid, lo


def _ssd_kernel(xbc_ref, dt_ref, dtn_ref, sz_ref, dtb_ref, alog_ref, dskip_ref, nw_ref, o_ref, xc_ref,
                y_ref, st_ref, dec_ref):
    L = SSM_CHUNK
    P = SSM_HEAD_DIM
    N = SSM_STATE
    d_inner = SSM_HEADS * P
    gn = SSM_GROUPS * N
    hpg = SSM_HEADS // SSM_GROUPS
    c = pl.program_id(1)

    @pl.when(c == 0)
    def _():
        st_ref[...] = jnp.zeros_like(st_ref)

    cstep = 512
    for c0 in range(0, d_inner + 2 * gn, cstep):
        xc_ref[:, c0:c0 + cstep] = _silu(xbc_ref[:, c0:c0 + cstep])

    ri = lax.broadcasted_iota(jnp.int32, (L, L), 0)
    ci = lax.broadcasted_iota(jnp.int32, (L, L), 1)
    causal = ri >= ci

    def prepare(dtr_ref):
        dt = _softplus(dtr_ref[...] + dtb_ref[...])
        la = dt * (-jnp.exp(alog_ref[...]))
        tri = jnp.where(causal, 1.0, 0.0).astype(BF16)
        la_h, la_m, la_l = _split3(la)
        cs = _dot(tri, la_h) + _dot(tri, la_m) + _dot(tri, la_l)
        cs2 = cs * LOG2E
        cs2_t = cs2.T
        row_t = cs2_t - jnp.log(dt.T) * LOG2E
        tw_t = jnp.exp2(cs2_t[:, L - 1:L] - row_t)
        da_t = jnp.broadcast_to(jnp.exp2(cs2_t[:, L - 1:L]), (L, L))
        return cs2, row_t, tw_t, da_t

    @pl.when(c == 0)
    def _():
        for k, v in enumerate(prepare(dt_ref)):
            dec_ref[k] = v

    nxt = prepare(dtn_ref)
    left = lax.broadcasted_iota(jnp.int32, (L, LANES), 1) < P
    left_row = lax.broadcasted_iota(jnp.int32, (1, LANES), 1) < P

    for g in range(SSM_GROUPS):
        bmat = xc_ref[:, d_inner + g * N:d_inner + (g + 1) * N]
        cmat = xc_ref[:, d_inner + gn + g * N:d_inner + gn + (g + 1) * N]
        cb = _dot_nt(cmat.astype(BF16), bmat.astype(BF16))
        bmat_t = bmat.T
        for pair in range(hpg // 2):
            jp = g * (hpg // 2) + pair
            lanes = slice(jp * LANES, (jp + 1) * LANES)
            xp = xc_ref[:, lanes].astype(BF16)
            st_old = st_ref[:, lanes]
            rhs = jnp.concatenate([xp, st_old.astype(BF16)], axis=0)
            ys, upds = [], []
            for h in (2 * jp, 2 * jp + 1):
                col = jnp.broadcast_to(dec_ref[0, :, h:h + 1], (L, L))
                seg = jnp.where(causal, col - dec_ref[1, h:h + 1, :], -jnp.inf)
                mmat = (cb * jnp.exp2(seg)).astype(BF16)
                cw = (cmat * jnp.exp2(col)).astype(BF16)
                ys.append(_dot(jnp.concatenate([mmat, cw], axis=1), rhs))
                bw = (bmat_t * dec_ref[2, h:h + 1, :]).astype(BF16)
                upds.append(_dot(bw, xp))
            y_ref[:, lanes] = jnp.where(left, ys[0], ys[1])
            da = jnp.where(left_row, dec_ref[3, 2 * jp:2 * jp + 1, 0:1],
                           dec_ref[3, 2 * jp + 1:2 * jp + 2, 0:1])
            st_ref[:, lanes] = st_old * da + jnp.where(left, upds[0], upds[1])

    gw = d_inner // SSM_GROUPS
    for g in range(SSM_GROUPS):
        cols = slice(g * gw, (g + 1) * gw)
        y = (y_ref[:, cols] + dskip_ref[:, cols] * xc_ref[:, cols]) * sz_ref[:, cols]
        ms = jnp.mean(y * y, axis=-1, keepdims=True)
        o_ref[:, cols] = (y * lax.rsqrt(ms + EPS) * nw_ref[:, cols]).astype(o_ref.dtype)

    for k, v in enumerate(nxt):
        dec_ref[k] = v


def _ssd(xbc, dt_raw, sz, dt_bias, a_log, d_skip, norm_w):
    b, s, wx = xbc.shape
    L = SSM_CHUNK
    d_inner = SSM_HEADS * SSM_HEAD_DIM
    pad = LANES - SSM_HEADS
    dtb = jnp.pad(dt_bias, (0, pad)).reshape(1, LANES)
    alog = jnp.pad(a_log, (0, pad)).reshape(1, LANES)
    dskip = jnp.repeat(d_skip, SSM_HEAD_DIM).reshape(1, d_inner)
    const = lambda bi, ci: (0, 0)
    nc = s // L
    return pl.pallas_call(
        _ssd_kernel,
        grid=(b, s // L),
        in_specs=[
            pl.BlockSpec((None, L, wx), lambda bi, ci: (bi, ci, 0)),
            pl.BlockSpec((None, L, LANES), lambda bi, ci: (bi, ci, 0)),
            pl.BlockSpec((None, L, LANES), lambda bi, ci: (bi, jnp.minimum(ci + 1, nc - 1), 0)),
            pl.BlockSpec((None, L, d_inner), lambda bi, ci: (bi, ci, 0)),
            pl.BlockSpec((1, LANES), const),
            pl.BlockSpec((1, LANES), const),
            pl.BlockSpec((1, d_inner), const),
            pl.BlockSpec((1, d_inner), const),
        ],
        out_specs=pl.BlockSpec((None, L, d_inner), lambda bi, ci: (bi, ci, 0)),
        out_shape=jax.ShapeDtypeStruct((b, s, d_inner), BF16),
        scratch_shapes=[
            pltpu.VMEM((L, wx), F32),
            pltpu.VMEM((L, d_inner), F32),
            pltpu.VMEM((SSM_STATE, d_inner), F32),
            pltpu.VMEM((4, L, L), F32),
        ],
        compiler_params=_cparams("parallel", "arbitrary"),
        name="ssd_mixer",
    )(xbc, dt_raw, dt_raw, sz, dtb, alog, dskip, norm_w.reshape(1, d_inner))


def kernel(x, ev_norm_w, ev_w_in, ev_dw_w, ev_dw_b, ev_ln_w, ev_ln_b, ev_w_out, od_norm_w,
           od_w_in, od_conv_w, od_conv_b, od_dt_bias, od_a_log, od_d, od_gnorm_w, od_w_out,
           final_norm_w):
    bsz, seq, d = x.shape
    t = bsz * seq
    x2 = x.reshape(t, d)
    tm = min(1024, seq)
    tp = min(512, seq)

    cw = ev_dw_w.shape[2]
    aw = ev_w_out.shape[1] - cw
    heads = aw // SB_HEAD_DIM
    w_in = ev_w_in[0]
    scale = SB_HEAD_DIM ** -0.5
    w0 = jnp.concatenate([w_in[:, :3 * cw], w_in[:, 3 * cw + 3 * aw:],
                          w_in[:, 3 * cw:3 * cw + aw] * (scale * LOG2E),
                          w_in[:, 3 * cw + aw:3 * cw + 3 * aw]], axis=1).astype(BF16)
    u, gates, qkv = _proj0(x2, ev_norm_w[0], w0, cw=cw, ng=cw + aw, nqkv=3 * aw, tm=tp)

    y_conv = _conv_branch(u.reshape(bsz, seq, cw), gates.reshape(bsz, seq, cw + aw),
                          ev_dw_w[0], ev_dw_b[0], ev_ln_w[0], ev_ln_b[0], ts=min(256, seq))
    tq = min(1024, seq)
    y_attn = _attention(qkv.reshape(bsz, seq, 3 * aw), gates.reshape(bsz, seq, cw + aw),
                        heads=heads, tq=tq, tk=min(256, tq))

    w_out = ev_w_out[0].astype(BF16)
    h1 = pl.pallas_call(
        _outproj2_kernel,
        grid=(t // tm,),
        in_specs=[
            pl.BlockSpec((tm, cw), lambda i: (i, 0)),
            pl.BlockSpec((tm, aw), lambda i: (i, 0)),
            pl.BlockSpec((cw, d), lambda i: (0, 0)),
            pl.BlockSpec((aw, d), lambda i: (0, 0)),
            pl.BlockSpec((tm, d), lambda i: (i, 0)),
        ],
        out_specs=pl.BlockSpec((tm, d), lambda i: (i, 0)),
        out_shape=jax.ShapeDtypeStruct((t, d), F32),
        compiler_params=_cparams("parallel"),
        name="outproj0",
    )(y_conv.reshape(t, cw), y_attn.reshape(t, aw), w_out[:cw], w_out[cw:], x2)

    d_inner = SSM_HEADS * SSM_HEAD_DIM
    wx = od_conv_w.shape[2]
    w1 = jnp.pad(od_w_in[0], ((0, 0), (0, LANES - SSM_HEADS))).astype(BF16)
    sz, xbc, dt_raw = _proj1(h1, od_norm_w[0], w1, od_conv_w[0], od_conv_b[0], dz=d_inner, wx=wx,
                             seq=seq, tm=tp)

    yn = _ssd(xbc.reshape(bsz, seq, wx), dt_raw.reshape(bsz, seq, LANES),
              sz.reshape(bsz, seq, d_inner), od_dt_bias[0], od_a_log[0], od_d[0], od_gnorm_w[0])

    out = pl.pallas_call(
        _outproj_norm_kernel,
        grid=(t // tm,),
        in_specs=[
            pl.BlockSpec((tm, d_inner), lambda i: (i, 0)),
            pl.BlockSpec((d_inner, d), lambda i: (0, 0)),
            pl.BlockSpec((tm, d), lambda i: (i, 0)),
            pl.BlockSpec((1, d), lambda i: (0, 0)),
        ],
        out_specs=pl.BlockSpec((tm, d), lambda i: (i, 0)),
        out_shape=jax.ShapeDtypeStruct((t, d), F32),
        compiler_params=_cparams("parallel"),
        name="outproj1_norm",
    )(yn.reshape(t, d_inner), od_w_out[0].astype(BF16), h1, final_norm_w.reshape(1, d))
    return out.reshape(bsz, seq, d)
```

```python
import functools

import jax
import jax.numpy as jnp
from jax import lax
from jax.experimental import pallas as pl
from jax.experimental.pallas import tpu as pltpu

F32 = jnp.float32
BF16 = jnp.bfloat16

EPS = 1e-6
LANES = 128
SUBLANES = 8
LOG2E = 1.4426950408889634
SB_HEAD_DIM = 128
CONF_KERNEL = 31
SSM_HEAD_DIM = 64
SSM_HEADS = 32
SSM_GROUPS = 4
SSM_STATE = 128
SSM_CONV = 4
SSM_CHUNK = 128
VMEM_LIMIT = 56 * 1024 * 1024


def _cparams(*sem):
    return pltpu.CompilerParams(dimension_semantics=sem, vmem_limit_bytes=VMEM_LIMIT)


def _dot(a, b):
    return jnp.dot(a, b, preferred_element_type=F32)


def _dot_nt(a, b):
    return lax.dot_general(a, b, (((1,), (1,)), ((), ())), preferred_element_type=F32)


def _silu(x):
    h = 0.5 * x
    return h * jnp.tanh(h) + h


def _softplus(x):
    return jnp.maximum(x, 0.0) + jnp.log(1.0 + jnp.exp(-jnp.abs(x)))


PROJ_COLS = 1024
PROJ_CONV_HALO = 8


def _rmsnorm_bf16(x_ref, nw_ref, xn_ref):
    rows = x_ref.shape[0]
    step = min(256, rows)
    for r in range(0, rows, step):
        x = x_ref[r:r + step, :]
        ms = jnp.mean(x * x, axis=-1, keepdims=True)
        xn_ref[r:r + step, :] = (x * lax.rsqrt(ms + EPS) * nw_ref[...]).astype(BF16)


def _resident(shape):
    return pl.BlockSpec(shape, lambda i: (0,) * len(shape), pipeline_mode=pl.Buffered(1))


def _proj0_kernel(x_ref, nw_ref, w_ref, u_ref, g_ref, qkv_ref, xn_ref):
    cw = u_ref.shape[1]
    ng = g_ref.shape[1]
    _rmsnorm_bf16(x_ref, nw_ref, xn_ref)
    xn = xn_ref[...]
    for c0 in range(0, cw, PROJ_COLS):
        half = 0.5 * _dot(xn, w_ref[:, c0:c0 + PROJ_COLS])
        gate = _dot(xn, w_ref[:, cw + c0:cw + c0 + PROJ_COLS])
        u_ref[:, c0:c0 + PROJ_COLS] = half * jnp.tanh(0.5 * gate) + half
    for c0 in range(0, ng, PROJ_COLS):
        g_ref[:, c0:c0 + PROJ_COLS] = _silu(_dot(xn, w_ref[:, 2 * cw + c0:2 * cw + c0 + PROJ_COLS]))
    for c0 in range(0, qkv_ref.shape[1], PROJ_COLS):
        lo = 2 * cw + ng + c0
        qkv_ref[:, c0:c0 + PROJ_COLS] = _dot(xn, w_ref[:, lo:lo + PROJ_COLS]).astype(BF16)


def _proj0(x2d, norm_w, w, *, cw, ng, nqkv, tm):
    t, d = x2d.shape
    row = lambda n: pl.BlockSpec((tm, n), lambda i: (i, 0))
    return pl.pallas_call(
        _proj0_kernel,
        grid=(t // tm,),
        in_specs=[row(d), _resident((1, d)), _resident(w.shape)],
        out_specs=[row(cw), row(ng), row(nqkv)],
        out_shape=[jax.ShapeDtypeStruct((t, cw), F32), jax.ShapeDtypeStruct((t, ng), F32),
                   jax.ShapeDtypeStruct((t, nqkv), BF16)],
        scratch_shapes=[pltpu.VMEM((tm, d), BF16)],
        compiler_params=_cparams("parallel"),
        name="proj0",
    )(x2d, norm_w.reshape(1, d), w)


def _proj1_kernel(x_ref, nw_ref, w_ref, cw_ref, cb_ref, sz_ref, xbc_ref, dt_ref, xn_ref, win_ref,
                  raw_ref, *, tiles_per_seq):
    s = pl.program_id(0)
    tm = x_ref.shape[0]
    dz = sz_ref.shape[1]
    wx = xbc_ref.shape[1]

    @pl.when(s == 0)
    def _():
        win_ref[...] = jnp.zeros_like(win_ref)

    off = PROJ_CONV_HALO - (SSM_CONV - 1)
    rstep = min(256, tm)

    for c0 in range(0, wx, PROJ_COLS):
        cols = slice(c0, c0 + PROJ_COLS)
        for r in range(0, tm, rstep):
            acc = jnp.broadcast_to(cb_ref[:, cols], (rstep, PROJ_COLS))
            for k in range(SSM_CONV):
                wk = cw_ref[k * SUBLANES:(k + 1) * SUBLANES, cols]
                acc = acc + (win_ref[r + off + k:r + off + k + rstep, cols]
                             * jnp.concatenate([wk] * (rstep // SUBLANES), axis=0))
            xbc_ref[r:r + rstep, cols] = acc

    _rmsnorm_bf16(x_ref, nw_ref, xn_ref)
    xn = xn_ref[...]
    for c0 in range(0, dz, PROJ_COLS):
        sz_ref[:, c0:c0 + PROJ_COLS] = _silu(_dot(xn, w_ref[:, c0:c0 + PROJ_COLS]))
    dt_ref[...] = _dot(xn, w_ref[:, dz + wx:])
    for c0 in range(0, wx, PROJ_COLS):
        raw_ref[:, c0:c0 + PROJ_COLS] = _dot(xn, w_ref[:, dz + c0:dz + c0 + PROJ_COLS])

    first = (s % tiles_per_seq) == 0
    win_ref[0:PROJ_CONV_HALO, :] = jnp.where(first, 0.0, win_ref[tm:tm + PROJ_CONV_HALO, :])
    win_ref[PROJ_CONV_HALO:, :] = raw_ref[...]


def _proj1(x2d, norm_w, w, conv_w, conv_b, *, dz, wx, seq, tm):
    t, d = x2d.shape
    assert seq % tm == 0
    nt = t // tm
    cur = lambda n: pl.BlockSpec((tm, n), lambda s: (jnp.minimum(s, nt - 1), 0))
    prev = lambda n: pl.BlockSpec((tm, n), lambda s: (jnp.maximum(s - 1, 0), 0))
    cw_rep = jnp.repeat(conv_w, SUBLANES, axis=0)
    return pl.pallas_call(
        functools.partial(_proj1_kernel, tiles_per_seq=seq // tm),
        grid=(nt + 1,),
        in_specs=[cur(d), _resident((1, d)), _resident(w.shape), _resident(cw_rep.shape),
                  _resident((1, wx))],
        out_specs=[cur(dz), prev(wx), cur(LANES)],
        out_shape=[jax.ShapeDtypeStruct((t, dz), F32), jax.ShapeDtypeStruct((t, wx), F32),
                   jax.ShapeDtypeStruct((t, LANES), F32)],
        scratch_shapes=[pltpu.VMEM((tm, d), BF16),
                        pltpu.VMEM((tm + PROJ_CONV_HALO, wx), F32),
                        pltpu.VMEM((tm, wx), F32)],
        compiler_params=_cparams("arbitrary"),
        name="proj1",
    )(x2d, norm_w.reshape(1, d), w, cw_rep, conv_b.reshape(1, wx))


def _outproj2_kernel(ya_ref, yb_ref, wa_ref, wb_ref, res_ref, o_ref):
    acc = _dot(ya_ref[...], wa_ref[...]) + _dot(yb_ref[...], wb_ref[...])
    o_ref[...] = acc + res_ref[...]


def _outproj_norm_kernel(y_ref, w_ref, res_ref, nw_ref, o_ref):
    h = _dot(y_ref[...], w_ref[...]) + res_ref[...]
    ms = jnp.mean(h * h, axis=-1, keepdims=True)
    o_ref[...] = h * lax.rsqrt(ms + EPS) * nw_ref[...]


CONV_HALO = 32
CONV_ROWS = 32


def _conv_branch_kernel(u_ref, halo_ref, g_ref, w_ref, b_ref, lnw_ref, lnb_ref, o_ref, win_ref,
                        sh_ref):
    ts = u_ref.shape[0]
    first = pl.program_id(1) == 0
    win_ref[0:CONV_HALO, :] = jnp.where(first, 0.0, halo_ref[...])
    win_ref[CONV_HALO:, :] = u_ref[...]
    nsh = sh_ref.shape[1]
    for r in range(1, SUBLANES):
        sh_ref[r - 1] = win_ref[r:r + nsh, :]
    off = CONV_HALO - (CONF_KERNEL - 1)
    for r in range(0, ts, CONV_ROWS):
        acc = jnp.broadcast_to(b_ref[...], (CONV_ROWS, u_ref.shape[1]))
        for k in range(CONF_KERNEL):
            res = (off + k) % SUBLANES
            base = r + off + k - res
            if res == 0:
                tap = win_ref[base:base + CONV_ROWS, :]
            else:
                tap = sh_ref[res - 1, base:base + CONV_ROWS, :]
            wk = w_ref[k * SUBLANES:(k + 1) * SUBLANES, :]
            acc = acc + tap * jnp.concatenate([wk] * (CONV_ROWS // SUBLANES), axis=0)
        mu = jnp.mean(acc, axis=-1, keepdims=True)
        xc = acc - mu
        var = jnp.mean(xc * xc, axis=-1, keepdims=True)
        y = xc * lax.rsqrt(var + EPS) * lnw_ref[...] + lnb_ref[...]
        o_ref[r:r + CONV_ROWS, :] = (_silu(y) * g_ref[r:r + CONV_ROWS, :]).astype(o_ref.dtype)


def _conv_branch(u, gates, dw_w, dw_b, ln_w, ln_b, *, ts):
    b, s, c = u.shape
    hb = ts // CONV_HALO
    w_rep = jnp.repeat(dw_w, SUBLANES, axis=0)
    return pl.pallas_call(
        _conv_branch_kernel,
        grid=(b, s // ts),
        in_specs=[
            pl.BlockSpec((None, ts, c), lambda bi, i: (bi, i, 0)),
            pl.BlockSpec((None, CONV_HALO, c), lambda bi, i: (bi, jnp.maximum(i * hb - 1, 0), 0)),
            pl.BlockSpec((None, ts, c), lambda bi, i: (bi, i, 0)),
            pl.BlockSpec((CONF_KERNEL * SUBLANES, c), lambda bi, i: (0, 0)),
            pl.BlockSpec((1, c), lambda bi, i: (0, 0)),
            pl.BlockSpec((1, c), lambda bi, i: (0, 0)),
            pl.BlockSpec((1, c), lambda bi, i: (0, 0)),
        ],
        out_specs=pl.BlockSpec((None, ts, c), lambda bi, i: (bi, i, 0)),
        out_shape=jax.ShapeDtypeStruct((b, s, c), BF16),
        scratch_shapes=[pltpu.VMEM((ts + CONV_HALO, c), F32),
                        pltpu.VMEM((SUBLANES - 1, ts + CONV_HALO - SUBLANES, c), F32)],
        compiler_params=_cparams("parallel", "arbitrary"),
        name="conv_branch",
    )(u, u, gates, w_rep, dw_b.reshape(1, c), ln_w.reshape(1, c), ln_b.reshape(1, c))


SKIP_LOG2 = -150.0


def _attn_kernel(q_ref, k_ref, v_ref, g_ref, o_ref, acc_ref, carry_ref, *, tq, tk, nh):
    i = pl.program_id(2)
    d = SB_HEAD_DIM
    acc_ref[...] = jnp.zeros_like(acc_ref)
    carry_ref[...] = jnp.zeros_like(carry_ref)
    rr = lax.broadcasted_iota(jnp.int32, (tk, tk), 0)
    cc = lax.broadcasted_iota(jnp.int32, (tk, tk), 1)
    upper = jnp.where(rr > cc, 1.0, 0.0).astype(BF16)

    def tile(hh, kstart, lo, hi, diag, gate=None):
        m = hi - lo
        hl = slice(hh * d, (hh + 1) * d)
        q = q_ref[lo:hi, hl]
        k = k_ref[pl.ds(kstart, tk), hl]
        v = v_ref[pl.ds(kstart, tk), hl]
        z = _dot_nt(q, k)
        sp = jnp.maximum(z, 0.0) + jnp.log(1.0 + jnp.exp2(-jnp.abs(z))) * LOG2E
        lsig = z - sp
        if diag:
            mask = (lax.broadcasted_iota(jnp.int32, (m, tk), 1)
                    < lax.broadcasted_iota(jnp.int32, (m, tk), 0))
            spm = jnp.where(mask, sp, 0.0)
        elif gate is not None:
            spm = jnp.where(gate, sp, 0.0)
        else:
            spm = sp
        carry = carry_ref[hh, lo:hi, :]
        later = jnp.tile(carry, (1, tk // LANES)) - _dot(spm.astype(BF16), upper)
        w = jnp.exp2(lsig + later)
        if diag:
            w = jnp.where(mask, w, 0.0)
        elif gate is not None:
            w = jnp.where(gate, w, 0.0)
        acc_ref[hh, lo:hi, :] += _dot(w.astype(BF16), v)
        carry_ref[hh, lo:hi, :] = carry - jnp.sum(spm, axis=1, keepdims=True)

    def alive(hh, lo, hi):
        return jnp.max(carry_ref[hh, lo:hi, :]) > SKIP_LOG2

    base = pl.multiple_of(i * tq, tq)
    for lo in range(tq - tk, -1, -tk):
        for hh in range(nh):
            tile(hh, base + lo, lo, min(lo + 2 * tk, tq), True)
    has_prev = i > 0
    prev = pl.multiple_of(jnp.maximum(base - tk, 0), tk)
    for hh in range(nh):
        tile(hh, prev, 0, tk, False, gate=has_prev)
    for hh in range(nh):
        for lo in range(tq - 3 * tk, -1, -tk):
            @pl.when(alive(hh, lo + 2 * tk, tq))
            def _():
                tile(hh, base + lo, lo + 2 * tk, tq, False)

        @pl.when(jnp.logical_and(has_prev, alive(hh, tk, tq)))
        def _():
            tile(hh, prev, tk, tq, False)

    def cond(st):
        j, live = st
        return jnp.logical_and(j >= 0, live > 0)

    for hh in range(nh):
        def body(st, hh=hh):
            j, _ = st
            ks = pl.multiple_of(j * tk, tk)
            top = alive(hh, 0, tk)
            rest = alive(hh, tk, tq)

            @pl.when(rest)
            def _():
                tile(hh, ks, 0, tq, False)

            @pl.when(jnp.logical_and(top, jnp.logical_not(rest)))
            def _():
                tile(hh, ks, 0, tk, False)

            return j - 1, jnp.logical_or(top, rest).astype(jnp.int32)

        lax.while_loop(cond, body, (i * (tq // tk) - 2, jnp.int32(1)))
        hl = slice(hh * d, (hh + 1) * d)
        o_ref[:, hl] = (acc_ref[hh] * g_ref[:, hl]).astype(o_ref.dtype)


ATTN_HEADS_PER_STEP = 2


def _attention(qkv, gates, *, heads, tq, tk):
    b, s, _ = qkv.shape
    nh = ATTN_HEADS_PER_STEP
    dd = nh * SB_HEAD_DIM
    hb = heads // nh
    gate_off = (gates.shape[2] - heads * SB_HEAD_DIM) // dd
    assert tq % tk == 0 and tq >= 2 * tk and s % tq == 0 and tk % LANES == 0 and heads % nh == 0
    return pl.pallas_call(
        functools.partial(_attn_kernel, tq=tq, tk=tk, nh=nh),
        grid=(b, hb, s // tq),
        in_specs=[
            pl.BlockSpec((None, tq, dd), lambda bi, h, i: (bi, i, h)),
            pl.BlockSpec((None, s, dd), lambda bi, h, i: (bi, 0, hb + h)),
            pl.BlockSpec((None, s, dd), lambda bi, h, i: (bi, 0, 2 * hb + h)),
            pl.BlockSpec((None, tq, dd), lambda bi, h, i: (bi, i, gate_off + h)),
        ],
        out_specs=pl.BlockSpec((None, tq, dd), lambda bi, h, i: (bi, i, h)),
        out_shape=jax.ShapeDtypeStruct((b, s, heads * SB_HEAD_DIM), BF16),
        scratch_shapes=[
            pltpu.VMEM((nh, tq, SB_HEAD_DIM), F32),
            pltpu.VMEM((nh, tq, LANES), F32),
        ],
        compiler_params=_cparams("parallel", "parallel", "arbitrary"),
        name="sb_attention",
    )(qkv, qkv, qkv, gates)


def _split3(x):
    hi = x.astype(BF16)
    r = x - hi.astype(F32)
    mid = r.astype(BF16)
    lo = (r - mid.astype(F32)).astype(BF16)
    return hi, mid, lo


DECAY_CHUNKS = 8


def _ssd_decay_kernel(dt_ref, dtb_ref, alog_ref, o_ref):
    L = SSM_CHUNK
    ri = lax.broadcasted_iota(jnp.int32, (L, L), 0)
    ci = lax.broadcasted_iota(jnp.int32, (L, L), 1)
    tri = jnp.where(ri >= ci, 1.0, 0.0).astype(BF16)
    for g in range(DECAY_CHUNKS):
        dt = _softplus(dt_ref[g * L:(g + 1) * L, :] + dtb_ref[...])
        la = dt * (-jnp.exp(alog_ref[...]))
        la_h, la_m, la_l = _split3(la)
        cs = _dot(tri, la_h) + _dot(tri, la_m) + _dot(tri, la_l)
        cs2 = cs * LOG2E
        cs2_t = cs2.T
        row_t = cs2_t - jnp.log(dt.T) * LOG2E
        o_ref[g, 0] = cs2
        o_ref[g, 1] = row_t
        o_ref[g, 2] = jnp.exp2(cs2_t[:, L - 1:L] - row_t)
        o_ref[g, 3] = jnp.broadcast_to(jnp.exp2(cs2_t[:, L - 1:L]), (L, L))


def _ssd_decay(dt_raw, dt_bias, a_log):
    b, s, _ = dt_raw.shape
    L = SSM_CHUNK
    nc = s // L
    g = DECAY_CHUNKS
    assert nc % g == 0
    pad = LANES - SSM_HEADS
    dtb = jnp.pad(dt_bias, (0, pad)).reshape(1, LANES)
    alog = jnp.pad(a_log, (0, pad)).reshape(1, LANES)
    const = lambda bi, ci: (0, 0)
    return pl.pallas_call(
        _ssd_decay_kernel,
        grid=(b, nc // g),
        in_specs=[pl.BlockSpec((None, g * L, LANES), lambda bi, ci: (bi, ci, 0)),
                  pl.BlockSpec((1, LANES), const), pl.BlockSpec((1, LANES), const)],
        out_specs=pl.BlockSpec((None, g, 4, L, L), lambda bi, ci: (bi, ci, 0, 0, 0)),
        out_shape=jax.ShapeDtypeStruct((b, nc, 4, L, L), F32),
        compiler_params=_cparams("parallel", "parallel"),
        name="ssd_decay",
    )(dt_raw, dtb, alog)


def _ssd_kernel(xbc_ref, dec_ref, sz_ref, dskip_ref, nw_ref, o_ref, xc_ref, y_ref, st_ref):
    L = SSM_CHUNK
    P = SSM_HEAD_DIM
    N = SSM_STATE
    d_inner = SSM_HEADS * P
    gn = SSM_GROUPS * N
    hpg = SSM_HEADS // SSM_GROUPS
    c = pl.program_id(1)

    @pl.when(c == 0)
    def _():
        st_ref[...] = jnp.zeros_like(st_ref)

    cstep = 512
    for c0 in range(0, d_inner + 2 * gn, cstep):
        xc_ref[:, c0:c0 + cstep] = _silu(xbc_ref[:, c0:c0 + cstep])

    ri = lax.broadcasted_iota(jnp.int32, (L, L), 0)
    ci = lax.broadcasted_iota(jnp.int32, (L, L), 1)
    causal = ri >= ci
    left = lax.broadcasted_iota(jnp.int32, (L, LANES), 1) < P
    left_row = lax.broadcasted_iota(jnp.int32, (1, LANES), 1) < P

    for g in range(SSM_GROUPS):
        bmat = xc_ref[:, d_inner + g * N:d_inner + (g + 1) * N]
        cmat = xc_ref[:, d_inner + gn + g * N:d_inner + gn + (g + 1) * N]
        cb = _dot_nt(cmat.astype(BF16), bmat.astype(BF16))
        bmat_t = bmat.T
        for pair in range(hpg // 2):
            jp = g * (hpg // 2) + pair
            lanes = slice(jp * LANES, (jp + 1) * LANES)
            xp = xc_ref[:, lanes].astype(BF16)
            st_old = st_ref[:, lanes]
            rhs = jnp.concatenate([xp, st_old.astype(BF16)], axis=0)
            ys, upds = [], []
            for h in (2 * jp, 2 * jp + 1):
                col = jnp.broadcast_to(dec_ref[0, :, h:h + 1], (L, L))
                seg = jnp.where(causal, col - dec_ref[1, h:h + 1, :], -jnp.inf)
                mmat = (cb * jnp.exp2(seg)).astype(BF16)
                cw = (cmat * jnp.exp2(col)).astype(BF16)
                ys.append(_dot(jnp.concatenate([mmat, cw], axis=1), rhs))
                bw = (bmat_t * dec_ref[2, h:h + 1, :]).astype(BF16)
                upds.append(_dot(bw, xp))
            y_ref[:, lanes] = jnp.where(left, ys[0], ys[1])
            da = jnp.where(left_row, dec_ref[3, 2 * jp:2 * jp + 1, 0:1],
                           dec_ref[3, 2 * jp + 1:2 * jp + 2, 0:1])
            st_ref[:, lanes] = st_old * da + jnp.where(left, upds[0], upds[1])

    gw = d_inner // SSM_GROUPS
    for g in range(SSM_GROUPS):
        cols = slice(g * gw, (g + 1) * gw)
        y = (y_ref[:, cols] + dskip_ref[:, cols] * xc_ref[:, cols]) * sz_ref[:, cols]
        ms = jnp.mean(y * y, axis=-1, keepdims=True)
        o_ref[:, cols] = (y * lax.rsqrt(ms + EPS) * nw_ref[:, cols]).astype(o_ref.dtype)


def _ssd(xbc, dec, sz, d_skip, norm_w):
    b, s, wx = xbc.shape
    L = SSM_CHUNK
    d_inner = SSM_HEADS * SSM_HEAD_DIM
    dskip = jnp.repeat(d_skip, SSM_HEAD_DIM).reshape(1, d_inner)
    const = lambda bi, ci: (0, 0)
    return pl.pallas_call(
        _ssd_kernel,
        grid=(b, s // L),
        in_specs=[
            pl.BlockSpec((None, L, wx), lambda bi, ci: (bi, ci, 0)),
            pl.BlockSpec((None, None, 4, L, L), lambda bi, ci: (bi, ci, 0, 0, 0)),
            pl.BlockSpec((None, L, d_inner), lambda bi, ci: (bi, ci, 0)),
            pl.BlockSpec((1, d_inner), const),
            pl.BlockSpec((1, d_inner), const),
        ],
        out_specs=pl.BlockSpec((None, L, d_inner), lambda bi, ci: (bi, ci, 0)),
        out_shape=jax.ShapeDtypeStruct((b, s, d_inner), BF16),
        scratch_shapes=[
            pltpu.VMEM((L, wx), F32),
            pltpu.VMEM((L, d_inner), F32),
            pltpu.VMEM((SSM_STATE, d_inner), F32),
        ],
        compiler_params=_cparams("parallel", "arbitrary"),
        name="ssd_mixer",
    )(xbc, dec, sz, dskip, norm_w.reshape(1, d_inner))


def kernel(x, ev_norm_w, ev_w_in, ev_dw_w, ev_dw_b, ev_ln_w, ev_ln_b, ev_w_out, od_norm_w,
           od_w_in, od_conv_w, od_conv_b, od_dt_bias, od_a_log, od_d, od_gnorm_w, od_w_out,
           final_norm_w):
    bsz, seq, d = x.shape
    t = bsz * seq
    x2 = x.reshape(t, d)
    tm = min(1024, seq)
    tp = min(512, seq)

    cw = ev_dw_w.shape[2]
    aw = ev_w_out.shape[1] - cw
    heads = aw // SB_HEAD_DIM
    w_in = ev_w_in[0]
    scale = SB_HEAD_DIM ** -0.5
    w0 = jnp.concatenate([w_in[:, :3 * cw], w_in[:, 3 * cw + 3 * aw:],
                          w_in[:, 3 * cw:3 * cw + aw] * (scale * LOG2E),
                          w_in[:, 3 * cw + aw:3 * cw + 3 * aw]], axis=1).astype(BF16)
    u, gates, qkv = _proj0(x2, ev_norm_w[0], w0, cw=cw, ng=cw + aw, nqkv=3 * aw, tm=tp)

    y_conv = _conv_branch(u.reshape(bsz, seq, cw), gates.reshape(bsz, seq, cw + aw),
                          ev_dw_w[0], ev_dw_b[0], ev_ln_w[0], ev_ln_b[0], ts=min(512, seq))
    tq = min(1024, seq)
    y_attn = _attention(qkv.reshape(bsz, seq, 3 * aw), gates.reshape(bsz, seq, cw + aw),
                        heads=heads, tq=tq, tk=min(256, tq))

    w_out = ev_w_out[0].astype(BF16)
    h1 = pl.pallas_call(
        _outproj2_kernel,
        grid=(t // tm,),
        in_specs=[
            pl.BlockSpec((tm, cw), lambda i: (i, 0)),
            pl.BlockSpec((tm, aw), lambda i: (i, 0)),
            pl.BlockSpec((cw, d), lambda i: (0, 0)),
            pl.BlockSpec((aw, d), lambda i: (0, 0)),
            pl.BlockSpec((tm, d), lambda i: (i, 0)),
        ],
        out_specs=pl.BlockSpec((tm, d), lambda i: (i, 0)),
        out_shape=jax.ShapeDtypeStruct((t, d), F32),
        compiler_params=_cparams("parallel"),
        name="outproj0",
    )(y_conv.reshape(t, cw), y_attn.reshape(t, aw), w_out[:cw], w_out[cw:], x2)

    d_inner = SSM_HEADS * SSM_HEAD_DIM
    wx = od_conv_w.shape[2]
    w1 = jnp.pad(od_w_in[0], ((0, 0), (0, LANES - SSM_HEADS))).astype(BF16)
    sz, xbc, dt_raw = _proj1(h1, od_norm_w[0], w1, od_conv_w[0], od_conv_b[0], dz=d_inner, wx=wx,
                             seq=seq, tm=tp)

    dec = _ssd_decay(dt_raw.reshape(bsz, seq, LANES), od_dt_bias[0], od_a_log[0])
    yn = _ssd(xbc.reshape(bsz, seq, wx), dec, sz.reshape(bsz, seq, d_inner), od_d[0], od_gnorm_w[0])

    out = pl.pallas_call(
        _outproj_norm_kernel,
        grid=(t // tm,),
        in_specs=[
            pl.BlockSpec((tm, d_inner), lambda i: (i, 0)),
            pl.BlockSpec((d_inner, d), lambda i: (0, 0)),
            pl.BlockSpec((tm, d), lambda i: (i, 0)),
            pl.BlockSpec((1, d), lambda i: (0, 0)),
        ],
        out_specs=pl.BlockSpec((tm, d), lambda i: (i, 0)),
        out_shape=jax.ShapeDtypeStruct((t, d), F32),
        compiler_params=_cparams("parallel"),
        name="outproj1_norm",
    )(yn.reshape(t, d_inner), od_w_out[0].astype(BF16), h1, final_norm_w.reshape(1, d))
    return out.reshape(bsz, seq, d)
```

```python
import functools

import jax
import jax.numpy as jnp
from jax import lax
from jax.experimental import pallas as pl
from jax.experimental.pallas import tpu as pltpu

F32 = jnp.float32
BF16 = jnp.bfloat16

EPS = 1e-6
LANES = 128
SUBLANES = 8
LOG2E = 1.4426950408889634
SB_HEAD_DIM = 128
CONF_KERNEL = 31
SSM_HEAD_DIM = 64
SSM_HEADS = 32
SSM_GROUPS = 4
SSM_STATE = 128
SSM_CONV = 4
SSM_CHUNK = 128
VMEM_LIMIT = 56 * 1024 * 1024


def _cparams(*sem):
    return pltpu.CompilerParams(dimension_semantics=sem, vmem_limit_bytes=VMEM_LIMIT)


def _dot(a, b):
    return jnp.dot(a, b, preferred_element_type=F32)


def _dot_nt(a, b):
    return lax.dot_general(a, b, (((1,), (1,)), ((), ())), preferred_element_type=F32)


def _silu(x):
    h = 0.5 * x
    return h * jnp.tanh(h) + h


def _softplus(x):
    return jnp.maximum(x, 0.0) + jnp.log(1.0 + jnp.exp(-jnp.abs(x)))


PROJ_COLS = 1024
PROJ_CONV_HALO = 8


def _rmsnorm_bf16(x_ref, nw_ref, xn_ref):
    rows = x_ref.shape[0]
    step = min(256, rows)
    for r in range(0, rows, step):
        x = x_ref[r:r + step, :]
        ms = jnp.mean(x * x, axis=-1, keepdims=True)
        xn_ref[r:r + step, :] = (x * lax.rsqrt(ms + EPS) * nw_ref[...]).astype(BF16)


def _resident(shape):
    return pl.BlockSpec(shape, lambda i: (0,) * len(shape), pipeline_mode=pl.Buffered(1))


def _proj0_kernel(x_ref, nw_ref, w_ref, u_ref, g_ref, qkv_ref, xn_ref):
    cw = u_ref.shape[1]
    ng = g_ref.shape[1]
    _rmsnorm_bf16(x_ref, nw_ref, xn_ref)
    xn = xn_ref[...]
    for c0 in range(0, cw, PROJ_COLS):
        half = 0.5 * _dot(xn, w_ref[:, c0:c0 + PROJ_COLS])
        gate = _dot(xn, w_ref[:, cw + c0:cw + c0 + PROJ_COLS])
        u_ref[:, c0:c0 + PROJ_COLS] = half * jnp.tanh(0.5 * gate) + half
    for c0 in range(0, ng, PROJ_COLS):
        g_ref[:, c0:c0 + PROJ_COLS] = _silu(_dot(xn, w_ref[:, 2 * cw + c0:2 * cw + c0 + PROJ_COLS]))
    for c0 in range(0, qkv_ref.shape[1], PROJ_COLS):
        lo = 2 * cw + ng + c0
        qkv_ref[:, c0:c0 + PROJ_COLS] = _dot(xn, w_ref[:, lo:lo + PROJ_COLS]).astype(BF16)


def _proj0(x2d, norm_w, w, *, cw, ng, nqkv, tm):
    t, d = x2d.shape
    row = lambda n: pl.BlockSpec((tm, n), lambda i: (i, 0))
    return pl.pallas_call(
        _proj0_kernel,
        grid=(t // tm,),
        in_specs=[row(d), _resident((1, d)), _resident(w.shape)],
        out_specs=[row(cw), row(ng), row(nqkv)],
        out_shape=[jax.ShapeDtypeStruct((t, cw), F32), jax.ShapeDtypeStruct((t, ng), F32),
                   jax.ShapeDtypeStruct((t, nqkv), BF16)],
        scratch_shapes=[pltpu.VMEM((tm, d), BF16)],
        compiler_params=_cparams("parallel"),
        name="proj0",
    )(x2d, norm_w.reshape(1, d), w)


def _proj1_kernel(x_ref, nw_ref, w_ref, cw_ref, cb_ref, sz_ref, xbc_ref, dt_ref, xn_ref, win_ref,
                  raw_ref, *, tiles_per_seq):
    s = pl.program_id(0)
    tm = x_ref.shape[0]
    dz = sz_ref.shape[1]
    wx = xbc_ref.shape[1]

    @pl.when(s == 0)
    def _():
        win_ref[...] = jnp.zeros_like(win_ref)

    off = PROJ_CONV_HALO - (SSM_CONV - 1)
    rstep = min(256, tm)

    for c0 in range(0, wx, PROJ_COLS):
        cols = slice(c0, c0 + PROJ_COLS)
        for r in range(0, tm, rstep):
            acc = jnp.broadcast_to(cb_ref[:, cols], (rstep, PROJ_COLS))
            for k in range(SSM_CONV):
                wk = cw_ref[k * SUBLANES:(k + 1) * SUBLANES, cols]
                acc = acc + (win_ref[r + off + k:r + off + k + rstep, cols]
                             * jnp.concatenate([wk] * (rstep // SUBLANES), axis=0))
            xbc_ref[r:r + rstep, cols] = acc

    _rmsnorm_bf16(x_ref, nw_ref, xn_ref)
    xn = xn_ref[...]
    for c0 in range(0, dz, PROJ_COLS):
        sz_ref[:, c0:c0 + PROJ_COLS] = _silu(_dot(xn, w_ref[:, c0:c0 + PROJ_COLS]))
    dt_ref[...] = _dot(xn, w_ref[:, dz + wx:])
    for c0 in range(0, wx, PROJ_COLS):
        raw_ref[:, c0:c0 + PROJ_COLS] = _dot(xn, w_ref[:, dz + c0:dz + c0 + PROJ_COLS])

    first = (s % tiles_per_seq) == 0
    win_ref[0:PROJ_CONV_HALO, :] = jnp.where(first, 0.0, win_ref[tm:tm + PROJ_CONV_HALO, :])
    win_ref[PROJ_CONV_HALO:, :] = raw_ref[...]


def _proj1(x2d, norm_w, w, conv_w, conv_b, *, dz, wx, seq, tm):
    t, d = x2d.shape
    assert seq % tm == 0
    nt = t // tm
    cur = lambda n: pl.BlockSpec((tm, n), lambda s: (jnp.minimum(s, nt - 1), 0))
    prev = lambda n: pl.BlockSpec((tm, n), lambda s: (jnp.maximum(s - 1, 0), 0))
    cw_rep = jnp.repeat(conv_w, SUBLANES, axis=0)
    return pl.pallas_call(
        functools.partial(_proj1_kernel, tiles_per_seq=seq // tm),
        grid=(nt + 1,),
        in_specs=[cur(d), _resident((1, d)), _resident(w.shape), _resident(cw_rep.shape),
                  _resident((1, wx))],
        out_specs=[cur(dz), prev(wx), cur(LANES)],
        out_shape=[jax.ShapeDtypeStruct((t, dz), F32), jax.ShapeDtypeStruct((t, wx), F32),
                   jax.ShapeDtypeStruct((t, LANES), F32)],
        scratch_shapes=[pltpu.VMEM((tm, d), BF16),
                        pltpu.VMEM((tm + PROJ_CONV_HALO, wx), F32),
                        pltpu.VMEM((tm, wx), F32)],
        compiler_params=_cparams("arbitrary"),
        name="proj1",
    )(x2d, norm_w.reshape(1, d), w, cw_rep, conv_b.reshape(1, wx))


def _outproj2_kernel(ya_ref, yb_ref, wa_ref, wb_ref, res_ref, o_ref):
    acc = _dot(ya_ref[...], wa_ref[...]) + _dot(yb_ref[...], wb_ref[...])
    o_ref[...] = acc + res_ref[...]


def _outproj_norm_kernel(y_ref, w_ref, res_ref, nw_ref, o_ref):
    h = _dot(y_ref[...], w_ref[...]) + res_ref[...]
    ms = jnp.mean(h * h, axis=-1, keepdims=True)
    o_ref[...] = h * lax.rsqrt(ms + EPS) * nw_ref[...]


CONV_HALO = 32
CONV_ROWS = 32


def _conv_branch_kernel(u_ref, halo_ref, g_ref, w_ref, b_ref, lnw_ref, lnb_ref, o_ref, win_ref,
                        sh_ref):
    ts = u_ref.shape[0]
    first = pl.program_id(1) == 0
    win_ref[0:CONV_HALO, :] = jnp.where(first, 0.0, halo_ref[...])
    win_ref[CONV_HALO:, :] = u_ref[...]
    nsh = sh_ref.shape[1]
    for r in range(1, SUBLANES):
        sh_ref[r - 1] = win_ref[r:r + nsh, :]
    off = CONV_HALO - (CONF_KERNEL - 1)
    for r in range(0, ts, CONV_ROWS):
        acc = jnp.broadcast_to(b_ref[...], (CONV_ROWS, u_ref.shape[1]))
        for k in range(CONF_KERNEL):
            res = (off + k) % SUBLANES
            base = r + off + k - res
            if res == 0:
                tap = win_ref[base:base + CONV_ROWS, :]
            else:
                tap = sh_ref[res - 1, base:base + CONV_ROWS, :]
            wk = w_ref[k * SUBLANES:(k + 1) * SUBLANES, :]
            acc = acc + tap * jnp.concatenate([wk] * (CONV_ROWS // SUBLANES), axis=0)
        mu = jnp.mean(acc, axis=-1, keepdims=True)
        xc = acc - mu
        var = jnp.mean(xc * xc, axis=-1, keepdims=True)
        y = xc * lax.rsqrt(var + EPS) * lnw_ref[...] + lnb_ref[...]
        o_ref[r:r + CONV_ROWS, :] = (_silu(y) * g_ref[r:r + CONV_ROWS, :]).astype(o_ref.dtype)


def _conv_branch(u, gates, dw_w, dw_b, ln_w, ln_b, *, ts):
    b, s, c = u.shape
    hb = ts // CONV_HALO
    w_rep = jnp.repeat(dw_w, SUBLANES, axis=0)
    return pl.pallas_call(
        _conv_branch_kernel,
        grid=(b, s // ts),
        in_specs=[
            pl.BlockSpec((None, ts, c), lambda bi, i: (bi, i, 0)),
            pl.BlockSpec((None, CONV_HALO, c), lambda bi, i: (bi, jnp.maximum(i * hb - 1, 0), 0)),
            pl.BlockSpec((None, ts, c), lambda bi, i: (bi, i, 0)),
            pl.BlockSpec((CONF_KERNEL * SUBLANES, c), lambda bi, i: (0, 0)),
            pl.BlockSpec((1, c), lambda bi, i: (0, 0)),
            pl.BlockSpec((1, c), lambda bi, i: (0, 0)),
            pl.BlockSpec((1, c), lambda bi, i: (0, 0)),
        ],
        out_specs=pl.BlockSpec((None, ts, c), lambda bi, i: (bi, i, 0)),
        out_shape=jax.ShapeDtypeStruct((b, s, c), BF16),
        scratch_shapes=[pltpu.VMEM((ts + CONV_HALO, c), F32),
                        pltpu.VMEM((SUBLANES - 1, ts + CONV_HALO - SUBLANES, c), F32)],
        compiler_params=_cparams("parallel", "arbitrary"),
        name="conv_branch",
    )(u, u, gates, w_rep, dw_b.reshape(1, c), ln_w.reshape(1, c), ln_b.reshape(1, c))


SKIP_LOG2 = -150.0


def _attn_kernel(q_ref, k_ref, v_ref, g_ref, o_ref, acc_ref, carry_ref, *, tq, tk, nh):
    i = pl.program_id(2)
    d = SB_HEAD_DIM
    acc_ref[...] = jnp.zeros_like(acc_ref)
    carry_ref[...] = jnp.zeros_like(carry_ref)
    rr = lax.broadcasted_iota(jnp.int32, (tk, tk), 0)
    cc = lax.broadcasted_iota(jnp.int32, (tk, tk), 1)
    upper = jnp.where(rr > cc, 1.0, 0.0).astype(BF16)

    def tile(hh, kstart, lo, hi, diag, gate=None):
        m = hi - lo
        hl = slice(hh * d, (hh + 1) * d)
        q = q_ref[lo:hi, hl]
        k = k_ref[pl.ds(kstart, tk), hl]
        v = v_ref[pl.ds(kstart, tk), hl]
        z = _dot_nt(q, k)
        sp = jnp.maximum(z, 0.0) + jnp.log(1.0 + jnp.exp2(-jnp.abs(z))) * LOG2E
        lsig = z - sp
        if diag:
            mask = (lax.broadcasted_iota(jnp.int32, (m, tk), 1)
                    < lax.broadcasted_iota(jnp.int32, (m, tk), 0))
            spm = jnp.where(mask, sp, 0.0)
        elif gate is not None:
            spm = jnp.where(gate, sp, 0.0)
        else:
            spm = sp
        carry = carry_ref[hh, lo:hi, :]
        later = jnp.tile(carry, (1, tk // LANES)) - _dot(spm.astype(BF16), upper)
        w = jnp.exp2(lsig + later)
        if diag:
            w = jnp.where(mask, w, 0.0)
        elif gate is not None:
            w = jnp.where(gate, w, 0.0)
        acc_ref[hh, lo:hi, :] += _dot(w.astype(BF16), v)
        carry_ref[hh, lo:hi, :] = carry - jnp.sum(spm, axis=1, keepdims=True)

    def alive(hh, lo, hi):
        return jnp.max(carry_ref[hh, lo:hi, 0:1]) > SKIP_LOG2

    base = pl.multiple_of(i * tq, tq)
    for lo in range(tq - tk, -1, -tk):
        for hh in range(nh):
            tile(hh, base + lo, lo, min(lo + 2 * tk, tq), True)
    has_prev = i > 0
    prev = pl.multiple_of(jnp.maximum(base - tk, 0), tk)
    for hh in range(nh):
        tile(hh, prev, 0, tk, False, gate=has_prev)
    for hh in range(nh):
        for lo in range(tq - 3 * tk, -1, -tk):
            @pl.when(alive(hh, lo + 2 * tk, tq))
            def _():
                tile(hh, base + lo, lo + 2 * tk, tq, False)

        @pl.when(jnp.logical_and(has_prev, alive(hh, tk, tq)))
        def _():
            tile(hh, prev, tk, tq, False)

    def cond(st):
        j, live = st
        return jnp.logical_and(j >= 0, live > 0)

    for hh in range(nh):
        def body(st, hh=hh):
            j, _ = st
            ks = pl.multiple_of(j * tk, tk)
            top = alive(hh, 0, tk)
            rest = alive(hh, tk, tq)

            @pl.when(rest)
            def _():
                tile(hh, ks, 0, tq, False)

            @pl.when(jnp.logical_and(top, jnp.logical_not(rest)))
            def _():
                tile(hh, ks, 0, tk, False)

            return j - 1, jnp.logical_or(top, rest).astype(jnp.int32)

        lax.while_loop(cond, body, (i * (tq // tk) - 2, jnp.int32(1)))
        hl = slice(hh * d, (hh + 1) * d)
        o_ref[:, hl] = (acc_ref[hh] * g_ref[:, hl]).astype(o_ref.dtype)


ATTN_HEADS_PER_STEP = 4


def _attention(qkv, gates, *, heads, tq, tk):
    b, s, _ = qkv.shape
    nh = ATTN_HEADS_PER_STEP
    dd = nh * SB_HEAD_DIM
    hb = heads // nh
    gate_off = (gates.shape[2] - heads * SB_HEAD_DIM) // dd
    assert tq % tk == 0 and tq >= 2 * tk and s % tq == 0 and tk % LANES == 0 and heads % nh == 0
    return pl.pallas_call(
        functools.partial(_attn_kernel, tq=tq, tk=tk, nh=nh),
        grid=(b, hb, s // tq),
        in_specs=[
            pl.BlockSpec((None, tq, dd), lambda bi, h, i: (bi, i, h)),
            pl.BlockSpec((None, s, dd), lambda bi, h, i: (bi, 0, hb + h)),
            pl.BlockSpec((None, s, dd), lambda bi, h, i: (bi, 0, 2 * hb + h)),
            pl.BlockSpec((None, tq, dd), lambda bi, h, i: (bi, i, gate_off + h)),
        ],
        out_specs=pl.BlockSpec((None, tq, dd), lambda bi, h, i: (bi, i, h)),
        out_shape=jax.ShapeDtypeStruct((b, s, heads * SB_HEAD_DIM), BF16),
        scratch_shapes=[
            pltpu.VMEM((nh, tq, SB_HEAD_DIM), F32),
            pltpu.VMEM((nh, tq, LANES), F32),
        ],
        compiler_params=_cparams("parallel", "parallel", "arbitrary"),
        name="sb_attention",
    )(qkv, qkv, qkv, gates)


def _split3(x):
    hi = x.astype(BF16)
    r = x - hi.astype(F32)
    mid = r.astype(BF16)
    lo = (r - mid.astype(F32)).astype(BF16)
    return hi, mid, lo


DECAY_CHUNKS = 8


def _ssd_decay_kernel(dt_ref, dtb_ref, alog_ref, o_ref):
    L = SSM_CHUNK
    ri = lax.broadcasted_iota(jnp.int32, (L, L), 0)
    ci = lax.broadcasted_iota(jnp.int32, (L, L), 1)
    tri = jnp.where(ri >= ci, 1.0, 0.0).astype(BF16)
    for g in range(DECAY_CHUNKS):
        dt = _softplus(dt_ref[g * L:(g + 1) * L, :] + dtb_ref[...])
        la = dt * (-jnp.exp(alog_ref[...]))
        la_h, la_m, la_l = _split3(la)
        cs = _dot(tri, la_h) + _dot(tri, la_m) + _dot(tri, la_l)
        cs2 = cs * LOG2E
        cs2_t = cs2.T
        row_t = cs2_t - jnp.log(dt.T) * LOG2E
        o_ref[g, 0] = cs2
        o_ref[g, 1] = row_t
        o_ref[g, 2] = jnp.exp2(cs2_t[:, L - 1:L] - row_t)
        o_ref[g, 3] = jnp.broadcast_to(jnp.exp2(cs2_t[:, L - 1:L]), (L, L))


def _ssd_decay(dt_raw, dt_bias, a_log):
    b, s, _ = dt_raw.shape
    L = SSM_CHUNK
    nc = s // L
    g = DECAY_CHUNKS
    assert nc % g == 0
    pad = LANES - SSM_HEADS
    dtb = jnp.pad(dt_bias, (0, pad)).reshape(1, LANES)
    alog = jnp.pad(a_log, (0, pad)).reshape(1, LANES)
    const = lambda bi, ci: (0, 0)
    return pl.pallas_call(
        _ssd_decay_kernel,
        grid=(b, nc // g),
        in_specs=[pl.BlockSpec((None, g * L, LANES), lambda bi, ci: (bi, ci, 0)),
                  pl.BlockSpec((1, LANES), const), pl.BlockSpec((1, LANES), const)],
        out_specs=pl.BlockSpec((None, g, 4, L, L), lambda bi, ci: (bi, ci, 0, 0, 0)),
        out_shape=jax.ShapeDtypeStruct((b, nc, 4, L, L), F32),
        compiler_params=_cparams("parallel", "parallel"),
        name="ssd_decay",
    )(dt_raw, dtb, alog)


SSD_CHUNKS_PER_STEP = 4


def _ssd_kernel(xbc_ref, dec_ref, sz_ref, dskip_ref, nw_ref, o_ref, xc_ref, y_ref, st_ref):
    L = SSM_CHUNK
    P = SSM_HEAD_DIM
    N = SSM_STATE
    d_inner = SSM_HEADS * P
    gn = SSM_GROUPS * N
    hpg = SSM_HEADS // SSM_GROUPS
    c = pl.program_id(1)

    @pl.when(c == 0)
    def _():
        st_ref[...] = jnp.zeros_like(st_ref)

    cstep = 512
    for c0 in range(0, d_inner + 2 * gn, cstep):
        xc_ref[:, c0:c0 + cstep] = _silu(xbc_ref[:, c0:c0 + cstep])

    ri = lax.broadcasted_iota(jnp.int32, (L, L), 0)
    ci = lax.broadcasted_iota(jnp.int32, (L, L), 1)
    causal = ri >= ci
    left = lax.broadcasted_iota(jnp.int32, (L, LANES), 1) < P
    left_row = lax.broadcasted_iota(jnp.int32, (1, LANES), 1) < P

    for sub in range(SSD_CHUNKS_PER_STEP):
        rows = slice(sub * L, (sub + 1) * L)
        for g in range(SSM_GROUPS):
            bmat = xc_ref[rows, d_inner + g * N:d_inner + (g + 1) * N]
            cmat = xc_ref[rows, d_inner + gn + g * N:d_inner + gn + (g + 1) * N]
            cb = _dot_nt(cmat.astype(BF16), bmat.astype(BF16))
            bmat_t = bmat.T
            for pair in range(hpg // 2):
                jp = g * (hpg // 2) + pair
                lanes = slice(jp * LANES, (jp + 1) * LANES)
                xp = xc_ref[rows, lanes].astype(BF16)
                st_old = st_ref[:, lanes]
                rhs = jnp.concatenate([xp, st_old.astype(BF16)], axis=0)
                ys, upds = [], []
                for h in (2 * jp, 2 * jp + 1):
                    col = jnp.broadcast_to(dec_ref[sub, 0, :, h:h + 1], (L, L))
                    seg = jnp.where(causal, col - dec_ref[sub, 1, h:h + 1, :], -jnp.inf)
                    mmat = (cb * jnp.exp2(seg)).astype(BF16)
                    cw = (cmat * jnp.exp2(col)).astype(BF16)
                    ys.append(_dot(jnp.concatenate([mmat, cw], axis=1), rhs))
                    bw = (bmat_t * dec_ref[sub, 2, h:h + 1, :]).astype(BF16)
                    upds.append(_dot(bw, xp))
                y_ref[rows, lanes] = jnp.where(left, ys[0], ys[1])
                da = jnp.where(left_row, dec_ref[sub, 3, 2 * jp:2 * jp + 1, 0:1],
                               dec_ref[sub, 3, 2 * jp + 1:2 * jp + 2, 0:1])
                st_ref[:, lanes] = st_old * da + jnp.where(left, upds[0], upds[1])

    gw = d_inner // SSM_GROUPS
    for g in range(SSM_GROUPS):
        cols = slice(g * gw, (g + 1) * gw)
        y = (y_ref[:, cols] + dskip_ref[:, cols] * xc_ref[:, cols]) * sz_ref[:, cols]
        ms = jnp.mean(y * y, axis=-1, keepdims=True)
        o_ref[:, cols] = (y * lax.rsqrt(ms + EPS) * nw_ref[:, cols]).astype(o_ref.dtype)


def _ssd(xbc, dec, sz, d_skip, norm_w):
    b, s, wx = xbc.shape
    k = SSD_CHUNKS_PER_STEP
    rows = k * SSM_CHUNK
    d_inner = SSM_HEADS * SSM_HEAD_DIM
    assert s % rows == 0
    dskip = jnp.repeat(d_skip, SSM_HEAD_DIM).reshape(1, d_inner)
    const = lambda bi, ci: (0, 0)
    return pl.pallas_call(
        _ssd_kernel,
        grid=(b, s // rows),
        in_specs=[
            pl.BlockSpec((None, rows, wx), lambda bi, ci: (bi, ci, 0)),
            pl.BlockSpec((None, k, 4, SSM_CHUNK, SSM_CHUNK), lambda bi, ci: (bi, ci, 0, 0, 0)),
            pl.BlockSpec((None, rows, d_inner), lambda bi, ci: (bi, ci, 0)),
            pl.BlockSpec((1, d_inner), const),
            pl.BlockSpec((1, d_inner), const),
        ],
        out_specs=pl.BlockSpec((None, rows, d_inner), lambda bi, ci: (bi, ci, 0)),
        out_shape=jax.ShapeDtypeStruct((b, s, d_inner), BF16),
        scratch_shapes=[
            pltpu.VMEM((rows, wx), F32),
            pltpu.VMEM((rows, d_inner), F32),
            pltpu.VMEM((SSM_STATE, d_inner), F32),
        ],
        compiler_params=_cparams("parallel", "arbitrary"),
        name="ssd_mixer",
    )(xbc, dec, sz, dskip, norm_w.reshape(1, d_inner))


def kernel(x, ev_norm_w, ev_w_in, ev_dw_w, ev_dw_b, ev_ln_w, ev_ln_b, ev_w_out, od_norm_w,
           od_w_in, od_conv_w, od_conv_b, od_dt_bias, od_a_log, od_d, od_gnorm_w, od_w_out,
           final_norm_w):
    bsz, seq, d = x.shape
    t = bsz * seq
    x2 = x.reshape(t, d)
    tm = min(1024, seq)
    tp = min(512, seq)

    cw = ev_dw_w.shape[2]
    aw = ev_w_out.shape[1] - cw
    heads = aw // SB_HEAD_DIM
    w_in = ev_w_in[0]
    scale = SB_HEAD_DIM ** -0.5
    w0 = jnp.concatenate([w_in[:, :3 * cw], w_in[:, 3 * cw + 3 * aw:],
                          w_in[:, 3 * cw:3 * cw + aw] * (scale * LOG2E),
                          w_in[:, 3 * cw + aw:3 * cw + 3 * aw]], axis=1).astype(BF16)
    u, gates, qkv = _proj0(x2, ev_norm_w[0], w0, cw=cw, ng=cw + aw, nqkv=3 * aw, tm=tp)

    y_conv = _conv_branch(u.reshape(bsz, seq, cw), gates.reshape(bsz, seq, cw + aw),
                          ev_dw_w[0], ev_dw_b[0], ev_ln_w[0], ev_ln_b[0], ts=min(512, seq))
    tq = min(1024, seq)
    y_attn = _attention(qkv.reshape(bsz, seq, 3 * aw), gates.reshape(bsz, seq, cw + aw),
                        heads=heads, tq=tq, tk=min(256, tq))

    w_out = ev_w_out[0].astype(BF16)
    h1 = pl.pallas_call(
        _outproj2_kernel,
        grid=(t // tm,),
        in_specs=[
            pl.BlockSpec((tm, cw), lambda i: (i, 0)),
            pl.BlockSpec((tm, aw), lambda i: (i, 0)),
            pl.BlockSpec((cw, d), lambda i: (0, 0)),
            pl.BlockSpec((aw, d), lambda i: (0, 0)),
            pl.BlockSpec((tm, d), lambda i: (i, 0)),
        ],
        out_specs=pl.BlockSpec((tm, d), lambda i: (i, 0)),
        out_shape=jax.ShapeDtypeStruct((t, d), F32),
        compiler_params=_cparams("parallel"),
        name="outproj0",
    )(y_conv.reshape(t, cw), y_attn.reshape(t, aw), w_out[:cw], w_out[cw:], x2)

    d_inner = SSM_HEADS * SSM_HEAD_DIM
    wx = od_conv_w.shape[2]
    w1 = jnp.pad(od_w_in[0], ((0, 0), (0, LANES - SSM_HEADS))).astype(BF16)
    sz, xbc, dt_raw = _proj1(h1, od_norm_w[0], w1, od_conv_w[0], od_conv_b[0], dz=d_inner, wx=wx,
                             seq=seq, tm=tp)

    dec = _ssd_decay(dt_raw.reshape(bsz, seq, LANES), od_dt_bias[0], od_a_log[0])
    yn = _ssd(xbc.reshape(bsz, seq, wx), dec, sz.reshape(bsz, seq, d_inner), od_d[0], od_gnorm_w[0])

    out = pl.pallas_call(
        _outproj_norm_kernel,
        grid=(t // tm,),
        in_specs=[
            pl.BlockSpec((tm, d_inner), lambda i: (i, 0)),
            pl.BlockSpec((d_inner, d), lambda i: (0, 0)),
            pl.BlockSpec((tm, d), lambda i: (i, 0)),
            pl.BlockSpec((1, d), lambda i: (0, 0)),
        ],
        out_specs=pl.BlockSpec((tm, d), lambda i: (i, 0)),
        out_shape=jax.ShapeDtypeStruct((t, d), F32),
        compiler_params=_cparams("parallel"),
        name="outproj1_norm",
    )(yn.reshape(t, d_inner), od_w_out[0].astype(BF16), h1, final_norm_w.reshape(1, d))
    return out.reshape(bsz, seq, d)
```
